```python
import math
import jax, jax.numpy as jnp
from jax import lax
import numpy as np

D_MODEL = 1024
BATCH = 8
SEQ = 2048
DEPTH = 1
DEC_BATCH = 128
DEC_SEQ = 8
PAST_LEN = 16384
PAGE_SIZE = 128

CHUNK = 128
W_A = D_MODEL
N_GROUPS_A = 4
GROUP_A = W_A // N_GROUPS_A
W_B = D_MODEL
CONV_W = 31
D_FF = ((8 * D_MODEL // 3 + 255) // 256) * 256
IN_W = 2 * W_A + 2 * W_B + 2 * D_MODEL
EPS = 1e-6

kernel_name = "gated_gmlp_conformer_hybrid_step"


def rmsnorm(x, g):
    xf = x.astype(jnp.float32)
    y = xf * lax.rsqrt(jnp.mean(xf * xf, axis=-1, keepdims=True) + EPS)
    return (y * g.astype(jnp.float32)).astype(x.dtype)


def layernorm(x, g, b):
    xf = x.astype(jnp.float32)
    mu = jnp.mean(xf, axis=-1, keepdims=True)
    var = jnp.mean(jnp.square(xf - mu), axis=-1, keepdims=True)
    y = (xf - mu) * lax.rsqrt(var + EPS)
    return (y * g.astype(jnp.float32) + b.astype(jnp.float32)).astype(x.dtype)


def spatial_gating(u, v, w_s, b_s):
    B, S, _ = v.shape
    L = min(CHUNK, S)
    n_chunks = S // L
    vr = v.reshape(B, n_chunks, L, N_GROUPS_A, GROUP_A)
    mask = jnp.tril(jnp.ones((L, L), dtype=bool))
    w = jnp.where(mask[None], w_s[:, :L, :L], jnp.zeros((), w_s.dtype))
    mixed = jnp.einsum('gts,bcsgd->bctgd', w, vr)
    mixed = mixed + jnp.transpose(b_s[:, :L])[None, None, :, :, None]
    return u * mixed.reshape(B, S, W_A)


def causal_depthwise_conv(h, buf, conv_w, conv_b):
    full = jnp.concatenate([buf, h], axis=1)
    new_buf = full[:, -(CONV_W - 1):]
    out = lax.conv_general_dilated(
        full, conv_w[:, None, :], window_strides=(1,), padding='VALID',
        dimension_numbers=('NWC', 'WIO', 'NWC'), feature_group_count=W_B)
    return out + conv_b, new_buf


def mixer(h, conv_buf, w_in, b_in, ln_v_g, ln_v_b, w_s, b_s, w_pa, b_pa,
          conv_w, conv_b, ln_c_g, ln_c_b, w_pb, b_pb, w_o):
    S = h.shape[1]
    proj = h @ w_in + b_in
    z, glu_in, gates = jnp.split(proj, [2 * W_A, 2 * W_A + 2 * W_B], axis=-1)
    z = jax.nn.gelu(z)
    u, v = jnp.split(z, 2, axis=-1)
    v = layernorm(v, ln_v_g, ln_v_b)
    a = spatial_gating(u, v, w_s, b_s) @ w_pa + b_pa
    last_start = ((S - 1) // CHUNK) * CHUNK
    chunk_v = v[:, last_start:]
    ga, gb = jnp.split(glu_in, 2, axis=-1)
    glu = ga * jax.nn.sigmoid(gb)
    if conv_buf is None:
        conv_buf = jnp.zeros((h.shape[0], CONV_W - 1, W_B), glu.dtype)
    cv, new_buf = causal_depthwise_conv(glu, conv_buf, conv_w, conv_b)
    bb = jax.nn.silu(layernorm(cv, ln_c_g, ln_c_b)) @ w_pb + b_pb
    g_a, g_b = jnp.split(gates, 2, axis=-1)
    merged = jax.nn.sigmoid(g_a) * a + jax.nn.sigmoid(g_b) * bb
    return merged @ w_o, new_buf, chunk_v


def layer(x, c, conv_buf, w_ada, b_ada, g_norm1, w_in, b_in, ln_v_g, ln_v_b, w_s, b_s,
          w_pa, b_pa, conv_w, conv_b, ln_c_g, ln_c_b, w_pb, b_pb, w_o,
          g_norm2, w_ffn_in, w_ffn_out):
    mod = jax.nn.silu(c) @ w_ada + b_ada
    sh1, sc1, gt1, sh2, sc2, gt2 = [m[:, None, :] for m in jnp.split(mod, 6, axis=-1)]
    h = rmsnorm(x, g_norm1) * (1.0 + sc1) + sh1
    mix, new_buf, chunk_v = mixer(h, conv_buf, w_in, b_in, ln_v_g, ln_v_b, w_s, b_s, w_pa, b_pa,
                                  conv_w, conv_b, ln_c_g, ln_c_b, w_pb, b_pb, w_o)
    x = x + gt1 * mix
    h = rmsnorm(x, g_norm2) * (1.0 + sc2) + sh2
    gate, up = jnp.split(h @ w_ffn_in, 2, axis=-1)
    x = x + gt2 * ((jax.nn.silu(gate) * up) @ w_ffn_out)
    return x, new_buf, chunk_v


def setup_inputs(seed: int = 0) -> dict:
    key = jax.random.key(seed)
    ks = iter(jax.random.split(key, 40))
    f32 = jnp.float32

    def nrm(shape, scale):
        return jax.random.normal(next(ks), shape, f32) * scale

    def gain(shape):
        return 1.0 + 0.05 * jax.random.normal(next(ks), shape, f32)

    D = D_MODEL
    return {
        "x_prompt": nrm((BATCH, SEQ, D), 1.0),
        "x_sample": nrm((DEC_BATCH, DEC_SEQ, D), 1.0),
        "c_prompt": nrm((BATCH, D), 1.0),
        "c_sample": nrm((DEC_BATCH, D), 1.0),
        "cache_conv": nrm((DEPTH, DEC_BATCH, CONV_W - 1, W_B), 0.5),
        "w_ada": nrm((DEPTH, D, 6 * D), 0.2 * D ** -0.5),
        "b_ada": nrm((DEPTH, 6 * D), 0.02),
        "g_norm1": gain((DEPTH, D)),
        "w_in": nrm((DEPTH, D, IN_W), D ** -0.5),
        "b_in": nrm((DEPTH, IN_W), 0.02),
        "ln_v_g": gain((DEPTH, W_A)),
        "ln_v_b": nrm((DEPTH, W_A), 0.02),
        "w_s": nrm((DEPTH, N_GROUPS_A, CHUNK, CHUNK), CHUNK ** -0.5),
        "b_s": gain((DEPTH, N_GROUPS_A, CHUNK)),
        "w_pa": nrm((DEPTH, W_A, D), W_A ** -0.5),
        "b_pa": nrm((DEPTH, D), 0.02),
        "conv_w": nrm((DEPTH, CONV_W, W_B), CONV_W ** -0.5),
        "conv_b": nrm((DEPTH, W_B), 0.02),
        "ln_c_g": gain((DEPTH, W_B)),
        "ln_c_b": nrm((DEPTH, W_B), 0.02),
        "w_pb": nrm((DEPTH, W_B, D), W_B ** -0.5),
        "b_pb": nrm((DEPTH, D), 0.02),
        "w_o": nrm((DEPTH, D, D), D ** -0.5),
        "g_norm2": gain((DEPTH, D)),
        "w_ffn_in": nrm((DEPTH, D, 2 * D_FF), D ** -0.5),
        "w_ffn_out": nrm((DEPTH, D_FF, D), D_FF ** -0.5),
        "g_final": gain((D,)),
    }


def reference(x_prompt, x_sample, c_prompt, c_sample, cache_conv, w_ada, b_ada, g_norm1,
              w_in, b_in, ln_v_g, ln_v_b, w_s, b_s, w_pa, b_pa, conv_w, conv_b,
              ln_c_g, ln_c_b, w_pb, b_pb, w_o, g_norm2, w_ffn_in, w_ffn_out, g_final):
    xp, xs = x_prompt, x_sample
    conv_p, conv_s, v_p, v_s = [], [], [], []
    for l in range(DEPTH):
        params = (w_ada[l], b_ada[l], g_norm1[l], w_in[l], b_in[l], ln_v_g[l], ln_v_b[l],
                  w_s[l], b_s[l], w_pa[l], b_pa[l], conv_w[l], conv_b[l], ln_c_g[l], ln_c_b[l],
                  w_pb[l], b_pb[l], w_o[l], g_norm2[l], w_ffn_in[l], w_ffn_out[l])
        xp, buf_p, cv_p = layer(xp, c_prompt, None, *params)
        xs, buf_s, cv_s = layer(xs, c_sample, cache_conv[l], *params)
        conv_p.append(buf_p)
        conv_s.append(buf_s)
        v_p.append(cv_p)
        v_s.append(cv_s)
    y_prompt = rmsnorm(xp, g_final)
    y_sample = rmsnorm(xs, g_final)
    new_conv_prompt = jnp.stack(conv_p)
    new_conv_sample = jnp.stack(conv_s)
    chunk_v_prompt = jnp.stack(v_p)
    chunk_v_sample = jnp.stack(v_s)
    return (y_prompt, y_sample, new_conv_prompt, new_conv_sample, chunk_v_prompt, chunk_v_sample)
```

```python
import functools

import jax
import jax.numpy as jnp
from jax import lax
from jax.experimental import pallas as pl
from jax.experimental.pallas import tpu as pltpu

D = 1024
CHUNK = 128
N_GROUPS = 4
GROUP = D // N_GROUPS
CONV_W = 31
HALO = CONV_W - 1
HALO_PAD = 32
D_FF = 2816
EPS = 1e-6
TM = 256
SUBLANES = 8
FFN_CHUNKS = ((0, 1024), (1024, 1024), (2048, 768))
CONV_STRIP = 16
VMEM_LIMIT_BYTES = 62 * 1024 * 1024

F32 = jnp.float32
BF16 = jnp.bfloat16


def _rms(x, g):
    ms = jnp.mean(x * x, axis=-1, keepdims=True)
    return x * lax.rsqrt(ms + EPS) * g


def _ln(x, g, b):
    mu = jnp.mean(x, axis=-1, keepdims=True)
    xc = x - mu
    var = jnp.mean(xc * xc, axis=-1, keepdims=True)
    return xc * lax.rsqrt(var + EPS) * g + b


def _dot(a, b):
    return jnp.dot(a, b, preferred_element_type=F32)


def _mod_kernel(c_ref, w_ref, b_ref, o_ref):
    c = c_ref[...]
    s = (c * jax.nn.sigmoid(c)).astype(BF16)
    o_ref[...] = _dot(s, w_ref[...].astype(BF16)) + b_ref[...]


def _layer_kernel(sample, *refs):
    if sample:
        (x_ref, mod_ref, cache_ref, w1_ref, w2_ref, *rest) = refs
    else:
        (x_ref, mod_ref, cw_ref, *rest) = refs
    (g1_ref, w_in_ref, b_in_ref, lnv_g_ref, lnv_b_ref, wmix_ref, bmix_ref,
     w_pa_ref, b_pa_ref, conv_b_ref, lnc_g_ref, lnc_b_ref, w_pb_ref, b_pb_ref,
     w_o_ref, g2_ref, w_fi_ref, w_fo_ref, gf_ref, *rest) = rest
    if sample:
        y_ref, glu_out_ref, cv_out_ref, mix_ref = rest
    else:
        y_ref, conv_out_ref, cv_out_ref, mix_ref, ext_ref, cvs_ref = rest

    x = x_ref[...]
    if sample:
        ns = TM // SUBLANES

        def mod_row(i):
            r = mod_ref[:, i:i + 1, :]
            return jnp.broadcast_to(r, (ns, SUBLANES, D)).reshape(TM, D)
    else:
        def mod_row(i):
            return mod_ref[i:i + 1, :]

    sh1, sc1, gt1, sh2, sc2, gt2 = [mod_row(i) for i in range(6)]

    hb = (_rms(x, g1_ref[...]) * (1.0 + sc1) + sh1).astype(BF16)

    def proj(k):
        lo = k * D
        return _dot(hb, w_in_ref[:, lo:lo + D]) + b_in_ref[:, lo:lo + D]

    u = jax.nn.gelu(proj(0))
    v = _ln(jax.nn.gelu(proj(1)), lnv_g_ref[...], lnv_b_ref[...])
    if sample:
        cv_out_ref[...] = v
    else:
        @pl.when(pl.program_id(1) == pl.num_programs(1) - 1)
        def _():
            cv_out_ref[...] = v[TM - CHUNK:, :]
    vb = v.astype(BF16)
    for c in range(TM // CHUNK):
        r0 = c * CHUNK
        for g in range(N_GROUPS):
            c0 = g * GROUP
            mix_ref[r0:r0 + CHUNK, c0:c0 + GROUP] = (
                _dot(wmix_ref[g], vb[r0:r0 + CHUNK, c0:c0 + GROUP])
                + bmix_ref[:, c0:c0 + GROUP])
    a = _dot((u * mix_ref[...]).astype(BF16), w_pa_ref[...]) + b_pa_ref[...]

    glu = proj(2) * jax.nn.sigmoid(proj(3))
    if sample:
        glu_out_ref[...] = glu
        glu3 = glu.reshape(ns, SUBLANES, D)
        acc = jnp.broadcast_to(conv_b_ref[...], (SUBLANES, D))[None]
        for j in range(HALO):
            acc = acc + cache_ref[:, j:j + 1, :] * w1_ref[j][None]
        for s in range(SUBLANES):
            acc = acc + glu3[:, s:s + 1, :] * w2_ref[s][None]
        cv = acc.reshape(TM, D)
    else:
        first = pl.program_id(1) == 0

        @pl.when(first)
        def _():
            ext_ref[0:HALO_PAD, :] = jnp.zeros((HALO_PAD, D), F32)

        ext_ref[HALO_PAD:HALO_PAD + TM, :] = glu

        @pl.when(pl.program_id(1) == pl.num_programs(1) - 1)
        def _():
            conv_out_ref[...] = glu[TM - HALO:, :]

        off = HALO_PAD - HALO
        for r0 in range(0, TM, CONV_STRIP):
            acc = jnp.broadcast_to(conv_b_ref[...], (CONV_STRIP, D))
            for k in range(CONV_W):
                lo = r0 + off + k
                acc = acc + ext_ref[lo:lo + CONV_STRIP, :] * cw_ref[k:k + 1, :]
            cvs_ref[r0:r0 + CONV_STRIP, :] = acc
        ext_ref[0:HALO_PAD, :] = ext_ref[TM:TM + HALO_PAD, :]
        cv = cvs_ref[...]
    cn = _ln(cv, lnc_g_ref[...], lnc_b_ref[...])
    bb = _dot((cn * jax.nn.sigmoid(cn)).astype(BF16), w_pb_ref[...]) + b_pb_ref[...]

    merged = jax.nn.sigmoid(proj(4)) * a + jax.nn.sigmoid(proj(5)) * bb
    x1 = x + gt1 * _dot(merged.astype(BF16), w_o_ref[...])

    h2 = (_rms(x1, g2_ref[...]) * (1.0 + sc2) + sh2).astype(BF16)
    ffn = None
    for lo, w in FFN_CHUNKS:
        gate = _dot(h2, w_fi_ref[:, lo:lo + w])
        up = _dot(h2, w_fi_ref[:, D_FF + lo:D_FF + lo + w])
        act = (gate * jax.nn.sigmoid(gate) * up).astype(BF16)
        part = _dot(act, w_fo_ref[lo:lo + w, :])
        ffn = part if ffn is None else ffn + part
    x2 = x1 + gt2 * ffn
    y_ref[...] = _rms(x2, gf_ref[...])


def _resident(shape):
    nd = len(shape)
    return pl.BlockSpec(shape, lambda *_: (0,) * nd, pipeline_mode=pl.Buffered(1))


def _modulation(c_all, w_ada, b_ada):
    n = c_all.shape[0]
    nb = 4
    bw = 6 * D // nb
    return pl.pallas_call(
        _mod_kernel,
        grid=(nb,),
        in_specs=[
            pl.BlockSpec((n, D), lambda i: (0, 0)),
            pl.BlockSpec((D, bw), lambda i: (0, i)),
            pl.BlockSpec((1, bw), lambda i: (0, i)),
        ],
        out_specs=pl.BlockSpec((n, bw), lambda i: (0, i)),
        out_shape=jax.ShapeDtypeStruct((n, 6 * D), F32),
        name="adaln_mod",
    )(c_all, w_ada, b_ada)


def kernel(x_prompt, x_sample, c_prompt, c_sample, cache_conv, w_ada, b_ada, g_norm1,
           w_in, b_in, ln_v_g, ln_v_b, w_s, b_s, w_pa, b_pa, conv_w, conv_b,
           ln_c_g, ln_c_b, w_pb, b_pb, w_o, g_norm2, w_ffn_in, w_ffn_out, g_final):
    nb_p, seq, _ = x_prompt.shape
    nb_s, dec_seq, _ = x_sample.shape
    assert dec_seq == SUBLANES and seq % TM == 0 and (nb_s * dec_seq) % TM == 0
    assert w_ada.shape[0] == 1

    mod = _modulation(jnp.concatenate([c_prompt, c_sample], axis=0), w_ada[0],
                      b_ada[0][None]).reshape(nb_p + nb_s, 6, D)
    mod_p, mod_s = mod[:nb_p], mod[nb_p:]

    row = lambda a: a[0][None].astype(F32)
    tril = jnp.tril(jnp.ones((CHUNK, CHUNK), bool))
    wmix_p = jnp.where(tril[None], w_s[0], 0.0).astype(BF16)
    bmix_p = jnp.repeat(b_s[0].T, GROUP, axis=1)
    reps = CHUNK // dec_seq
    tril_s = jnp.tril(jnp.ones((dec_seq, dec_seq), bool))
    w_small = jnp.where(tril_s[None], w_s[0][:, :dec_seq, :dec_seq], 0.0)
    wmix_s = jax.vmap(lambda m: jnp.kron(jnp.eye(reps, dtype=F32), m))(w_small).astype(BF16)
    bmix_s = jnp.tile(jnp.repeat(b_s[0][:, :dec_seq].T, GROUP, axis=1), (reps, 1))
    cw = conv_w[0]
    t_idx = jnp.arange(dec_seq)
    j_idx = jnp.arange(HALO)
    k1 = j_idx[:, None] - t_idx[None, :]
    w1 = jnp.where((k1 >= 0)[..., None], cw[jnp.clip(k1, 0, CONV_W - 1)], 0.0)
    k2 = HALO - t_idx[None, :] + t_idx[:, None]
    w2 = jnp.where((k2 <= HALO)[..., None], cw[jnp.clip(k2, 0, CONV_W - 1)], 0.0)

    def weights(wmix, bmix):
        return [
            row(g_norm1), w_in[0].astype(BF16), row(b_in), row(ln_v_g), row(ln_v_b),
            wmix, bmix, w_pa[0].astype(BF16), row(b_pa), row(conv_b), row(ln_c_g),
            row(ln_c_b), w_pb[0].astype(BF16), row(b_pb), w_o[0].astype(BF16),
            row(g_norm2), w_ffn_in[0].astype(BF16), w_ffn_out[0].astype(BF16),
            g_final[None].astype(F32),
        ]

    params = pltpu.CompilerParams(
        dimension_semantics=("arbitrary", "arbitrary"),
        vmem_limit_bytes=VMEM_LIMIT_BYTES)

    wl = weights(wmix_p, bmix_p)
    nt = seq // TM
    y_p, conv_p, cv_p = pl.pallas_call(
        functools.partial(_layer_kernel, False),
        grid=(nb_p, nt),
        in_specs=[
            pl.BlockSpec((None, TM, D), lambda b, j: (b, j, 0)),
            pl.BlockSpec((None, 6, D), lambda b, j: (b, 0, 0)),
            _resident(cw.shape),
        ] + [_resident(w.shape) for w in wl],
        out_specs=[
            pl.BlockSpec((None, TM, D), lambda b, j: (b, j, 0)),
            pl.BlockSpec((None, HALO, D), lambda b, j: (b, 0, 0)),
            pl.BlockSpec((None, CHUNK, D), lambda b, j: (b, 0, 0)),
        ],
        out_shape=[
            jax.ShapeDtypeStruct((nb_p, seq, D), F32),
            jax.ShapeDtypeStruct((nb_p, HALO, D), F32),
            jax.ShapeDtypeStruct((nb_p, CHUNK, D), F32),
        ],
        scratch_shapes=[
            pltpu.VMEM((TM, D), F32),
            pltpu.VMEM((HALO_PAD + TM, D), F32),
            pltpu.VMEM((TM, D), F32),
        ],
        compiler_params=params,
        name="layer_prompt",
    )(x_prompt, mod_p, cw, *wl)

    wl = weights(wmix_s, bmix_s)
    rows_s = nb_s * dec_seq
    ns = TM // dec_seq
    y_s, glu_s, cv_s = pl.pallas_call(
        functools.partial(_layer_kernel, True),
        grid=(1, rows_s // TM),
        in_specs=[
            pl.BlockSpec((TM, D), lambda b, j: (j, 0)),
            pl.BlockSpec((ns, 6, D), lambda b, j: (j, 0, 0)),
            pl.BlockSpec((ns, HALO, D), lambda b, j: (j, 0, 0)),
            _resident(w1.shape), _resident(w2.shape),
        ] + [_resident(w.shape) for w in wl],
        out_specs=[
            pl.BlockSpec((TM, D), lambda b, j: (j, 0)),
            pl.BlockSpec((TM, D), lambda b, j: (j, 0)),
            pl.BlockSpec((TM, D), lambda b, j: (j, 0)),
        ],
        out_shape=[jax.ShapeDtypeStruct((rows_s, D), F32)] * 3,
        scratch_shapes=[pltpu.VMEM((TM, D), F32)],
        compiler_params=params,
        name="layer_sample",
    )(x_sample.reshape(rows_s, D), mod_s, cache_conv[0], w1, w2, *wl)

    new_conv_s = jnp.concatenate(
        [cache_conv[0][:, dec_seq:, :], glu_s.reshape(nb_s, dec_seq, D)], axis=1)
    return (y_p, y_s.reshape(nb_s, dec_seq, D), conv_p[None], new_conv_s[None],
            cv_p[None], cv_s.reshape(1, nb_s, dec_seq, D))
```

```python
import functools

import jax
import jax.numpy as jnp
from jax import lax
from jax.experimental import pallas as pl
from jax.experimental.pallas import tpu as pltpu

D = 1024
CHUNK = 128
N_GROUPS = 4
GROUP = D // N_GROUPS
CONV_W = 31
HALO = CONV_W - 1
HALO_PAD = 32
LANES = 128
LANE_TILES = D // LANES
D_FF = 2816
EPS = 1e-6
TM = 256
SUBLANES = 8
FFN_CHUNKS = ((0, 1024), (1024, 1024), (2048, 768))
HALO_TILES = HALO_PAD // SUBLANES
VMEM_LIMIT_BYTES = 62 * 1024 * 1024

F32 = jnp.float32
BF16 = jnp.bfloat16


def _rms(x, g):
    ms = jnp.mean(x * x, axis=-1, keepdims=True)
    return x * lax.rsqrt(ms + EPS) * g


def _ln(x, g, b):
    mu = jnp.mean(x, axis=-1, keepdims=True)
    xc = x - mu
    var = jnp.mean(xc * xc, axis=-1, keepdims=True)
    return xc * lax.rsqrt(var + EPS) * g + b


def _dot(a, b):
    return jnp.dot(a, b, preferred_element_type=F32)


def _mod_kernel(c_ref, w_ref, b_ref, o_ref):
    c = c_ref[...]
    s = (c * jax.nn.sigmoid(c)).astype(BF16)
    o_ref[...] = _dot(s, w_ref[...].astype(BF16)) + b_ref[...]


def _layer_kernel(sample, *refs):
    if sample:
        (x_ref, mod_ref, cache_ref, w1_ref, w2_ref, *rest) = refs
    else:
        (x_ref, mod_ref, cwt_ref, cbt_ref, *rest) = refs
    (g1_ref, w_in_ref, b_in_ref, lnv_g_ref, lnv_b_ref, wmix_ref, bmix_ref,
     w_pa_ref, b_pa_ref, conv_b_ref, lnc_g_ref, lnc_b_ref, w_pb_ref, b_pb_ref,
     w_o_ref, g2_ref, w_fi_ref, w_fo_ref, gf_ref, *rest) = rest
    if sample:
        y_ref, glu_out_ref, cv_out_ref, mix_ref = rest
    else:
        y_ref, conv_out_ref, cv_out_ref, mix_ref, ext_ref, cvz_ref = rest

    x = x_ref[...]
    if sample:
        ns = TM // SUBLANES

        def mod_row(i):
            r = mod_ref[:, i:i + 1, :]
            return jnp.broadcast_to(r, (ns, SUBLANES, D)).reshape(TM, D)
    else:
        def mod_row(i):
            return mod_ref[i:i + 1, :]

    sh1, sc1, gt1, sh2, sc2, gt2 = [mod_row(i) for i in range(6)]

    hb = (_rms(x, g1_ref[...]) * (1.0 + sc1) + sh1).astype(BF16)

    def proj(k):
        lo = k * D
        return _dot(hb, w_in_ref[:, lo:lo + D]) + b_in_ref[:, lo:lo + D]

    u = jax.nn.gelu(proj(0))
    v = _ln(jax.nn.gelu(proj(1)), lnv_g_ref[...], lnv_b_ref[...])
    if sample:
        cv_out_ref[...] = v
    else:
        @pl.when(pl.program_id(1) == pl.num_programs(1) - 1)
        def _():
            cv_out_ref[...] = v[TM - CHUNK:, :]
    vb = v.astype(BF16)
    for c in range(TM // CHUNK):
        r0 = c * CHUNK
        for g in range(N_GROUPS):
            c0 = g * GROUP
            mix_ref[r0:r0 + CHUNK, c0:c0 + GROUP] = (
                _dot(wmix_ref[g], vb[r0:r0 + CHUNK, c0:c0 + GROUP])
                + bmix_ref[:, c0:c0 + GROUP])
    a = _dot((u * mix_ref[...]).astype(BF16), w_pa_ref[...]) + b_pa_ref[...]

    glu = proj(2) * jax.nn.sigmoid(proj(3))
    if sample:
        glu_out_ref[...] = glu
        glu3 = glu.reshape(ns, SUBLANES, D)
        acc = jnp.broadcast_to(conv_b_ref[...], (SUBLANES, D))[None]
        for j in range(HALO):
            acc = acc + cache_ref[:, j:j + 1, :] * w1_ref[j][None]
        for s in range(SUBLANES):
            acc = acc + glu3[:, s:s + 1, :] * w2_ref[s][None]
        cv = acc.reshape(TM, D)
    else:
        first = pl.program_id(1) == 0

        nt = TM // SUBLANES

        @pl.when(first)
        def _():
            ext_ref[0:HALO_TILES] = jnp.zeros((HALO_TILES, SUBLANES * LANE_TILES, LANES), F32)

        for s in range(LANE_TILES):
            ext_ref[HALO_TILES:HALO_TILES + nt, SUBLANES * s:SUBLANES * (s + 1), :] = (
                glu[:, LANES * s:LANES * (s + 1)].reshape(nt, SUBLANES, LANES))

        @pl.when(pl.program_id(1) == pl.num_programs(1) - 1)
        def _():
            conv_out_ref[...] = glu[TM - HALO:, :]

        def row_idx(tile, j):
            return (tile, pl.ds(j, LANE_TILES, stride=SUBLANES), slice(None))

        def conv_tile(i, carry):
            accs = [cbt_ref[...]] * SUBLANES
            for m in range(HALO_PAD - HALO, HALO_PAD + SUBLANES):
                a, j = divmod(m, SUBLANES)
                z = ext_ref[row_idx(i + a, j)]
                for t in range(SUBLANES):
                    k = m - t - (HALO_PAD - HALO)
                    if 0 <= k < CONV_W:
                        accs[t] = accs[t] + z * cwt_ref[SUBLANES * k:SUBLANES * (k + 1), :]
            for t in range(SUBLANES):
                cvz_ref[row_idx(i, t)] = accs[t]
            return carry

        lax.fori_loop(0, nt, conv_tile, 0)
        ext_ref[0:HALO_TILES] = ext_ref[nt:nt + HALO_TILES]
        cv = jnp.concatenate(
            [cvz_ref[:, SUBLANES * s:SUBLANES * (s + 1), :].reshape(TM, LANES)
             for s in range(LANE_TILES)], axis=1)
    cn = _ln(cv, lnc_g_ref[...], lnc_b_ref[...])
    bb = _dot((cn * jax.nn.sigmoid(cn)).astype(BF16), w_pb_ref[...]) + b_pb_ref[...]

    merged = jax.nn.sigmoid(proj(4)) * a + jax.nn.sigmoid(proj(5)) * bb
    x1 = x + gt1 * _dot(merged.astype(BF16), w_o_ref[...])

    h2 = (_rms(x1, g2_ref[...]) * (1.0 + sc2) + sh2).astype(BF16)
    ffn = None
    for lo, w in FFN_CHUNKS:
        gate = _dot(h2, w_fi_ref[:, lo:lo + w])
        up = _dot(h2, w_fi_ref[:, D_FF + lo:D_FF + lo + w])
        act = (gate * jax.nn.sigmoid(gate) * up).astype(BF16)
        part = _dot(act, w_fo_ref[lo:lo + w, :])
        ffn = part if ffn is None else ffn + part
    x2 = x1 + gt2 * ffn
    y_ref[...] = _rms(x2, gf_ref[...])


def _resident(shape):
    nd = len(shape)
    return pl.BlockSpec(shape, lambda *_: (0,) * nd, pipeline_mode=pl.Buffered(1))


def _modulation(c_all, w_ada, b_ada):
    n = c_all.shape[0]
    nb = 4
    bw = 6 * D // nb
    return pl.pallas_call(
        _mod_kernel,
        grid=(nb,),
        in_specs=[
            pl.BlockSpec((n, D), lambda i: (0, 0)),
            pl.BlockSpec((D, bw), lambda i: (0, i)),
            pl.BlockSpec((1, bw), lambda i: (0, i)),
        ],
        out_specs=pl.BlockSpec((n, bw), lambda i: (0, i)),
        out_shape=jax.ShapeDtypeStruct((n, 6 * D), F32),
        name="adaln_mod",
    )(c_all, w_ada, b_ada)


def kernel(x_prompt, x_sample, c_prompt, c_sample, cache_conv, w_ada, b_ada, g_norm1,
           w_in, b_in, ln_v_g, ln_v_b, w_s, b_s, w_pa, b_pa, conv_w, conv_b,
           ln_c_g, ln_c_b, w_pb, b_pb, w_o, g_norm2, w_ffn_in, w_ffn_out, g_final):
    nb_p, seq, _ = x_prompt.shape
    nb_s, dec_seq, _ = x_sample.shape
    assert dec_seq == SUBLANES and seq % TM == 0 and (nb_s * dec_seq) % TM == 0
    assert w_ada.shape[0] == 1

    mod = _modulation(jnp.concatenate([c_prompt, c_sample], axis=0), w_ada[0],
                      b_ada[0][None]).reshape(nb_p + nb_s, 6, D)
    mod_p, mod_s = mod[:nb_p], mod[nb_p:]

    row = lambda a: a[0][None].astype(F32)
    tril = jnp.tril(jnp.ones((CHUNK, CHUNK), bool))
    wmix_p = jnp.where(tril[None], w_s[0], 0.0).astype(BF16)
    bmix_p = jnp.repeat(b_s[0].T, GROUP, axis=1)
    reps = CHUNK // dec_seq
    tril_s = jnp.tril(jnp.ones((dec_seq, dec_seq), bool))
    w_small = jnp.where(tril_s[None], w_s[0][:, :dec_seq, :dec_seq], 0.0)
    wmix_s = jax.vmap(lambda m: jnp.kron(jnp.eye(reps, dtype=F32), m))(w_small).astype(BF16)
    bmix_s = jnp.tile(jnp.repeat(b_s[0][:, :dec_seq].T, GROUP, axis=1), (reps, 1))
    cw = conv_w[0]
    cwt = cw.reshape(CONV_W * LANE_TILES, LANES)
    cbt = conv_b[0].reshape(LANE_TILES, LANES)
    t_idx = jnp.arange(dec_seq)
    j_idx = jnp.arange(HALO)
    k1 = j_idx[:, None] - t_idx[None, :]
    w1 = jnp.where((k1 >= 0)[..., None], cw[jnp.clip(k1, 0, CONV_W - 1)], 0.0)
    k2 = HALO - t_idx[None, :] + t_idx[:, None]
    w2 = jnp.where((k2 <= HALO)[..., None], cw[jnp.clip(k2, 0, CONV_W - 1)], 0.0)

    def weights(wmix, bmix):
        return [
            row(g_norm1), w_in[0].astype(BF16), row(b_in), row(ln_v_g), row(ln_v_b),
            wmix, bmix, w_pa[0].astype(BF16), row(b_pa), row(conv_b), row(ln_c_g),
            row(ln_c_b), w_pb[0].astype(BF16), row(b_pb), w_o[0].astype(BF16),
            row(g_norm2), w_ffn_in[0].astype(BF16), w_ffn_out[0].astype(BF16),
            g_final[None].astype(F32),
        ]

    params = pltpu.CompilerParams(
        dimension_semantics=("arbitrary", "arbitrary"),
        vmem_limit_bytes=VMEM_LIMIT_BYTES)

    wl = weights(wmix_p, bmix_p)
    nt = seq // TM
    y_p, conv_p, cv_p = pl.pallas_call(
        functools.partial(_layer_kernel, False),
        grid=(nb_p, nt),
        in_specs=[
            pl.BlockSpec((None, TM, D), lambda b, j: (b, j, 0)),
            pl.BlockSpec((None, 6, D), lambda b, j: (b, 0, 0)),
            _resident(cwt.shape), _resident(cbt.shape),
        ] + [_resident(w.shape) for w in wl],
        out_specs=[
            pl.BlockSpec((None, TM, D), lambda b, j: (b, j, 0)),
            pl.BlockSpec((None, HALO, D), lambda b, j: (b, 0, 0)),
            pl.BlockSpec((None, CHUNK, D), lambda b, j: (b, 0, 0)),
        ],
        out_shape=[
            jax.ShapeDtypeStruct((nb_p, seq, D), F32),
            jax.ShapeDtypeStruct((nb_p, HALO, D), F32),
            jax.ShapeDtypeStruct((nb_p, CHUNK, D), F32),
        ],
        scratch_shapes=[
            pltpu.VMEM((TM, D), F32),
            pltpu.VMEM((HALO_TILES + TM // SUBLANES, SUBLANES * LANE_TILES, LANES), F32),
            pltpu.VMEM((TM // SUBLANES, SUBLANES * LANE_TILES, LANES), F32),
        ],
        compiler_params=params,
        name="layer_prompt",
    )(x_prompt, mod_p, cwt, cbt, *wl)

    wl = weights(wmix_s, bmix_s)
    rows_s = nb_s * dec_seq
    ns = TM // dec_seq
    y_s, glu_s, cv_s = pl.pallas_call(
        functools.partial(_layer_kernel, True),
        grid=(1, rows_s // TM),
        in_specs=[
            pl.BlockSpec((TM, D), lambda b, j: (j, 0)),
            pl.BlockSpec((ns, 6, D), lambda b, j: (j, 0, 0)),
            pl.BlockSpec((ns, HALO, D), lambda b, j: (j, 0, 0)),
            _resident(w1.shape), _resident(w2.shape),
        ] + [_resident(w.shape) for w in wl],
        out_specs=[
            pl.BlockSpec((TM, D), lambda b, j: (j, 0)),
            pl.BlockSpec((TM, D), lambda b, j: (j, 0)),
            pl.BlockSpec((TM, D), lambda b, j: (j, 0)),
        ],
        out_shape=[jax.ShapeDtypeStruct((rows_s, D), F32)] * 3,
        scratch_shapes=[pltpu.VMEM((TM, D), F32)],
        compiler_params=params,
        name="layer_sample",
    )(x_sample.reshape(rows_s, D), mod_s, cache_conv[0], w1, w2, *wl)

    new_conv_s = jnp.concatenate(
        [cache_conv[0][:, dec_seq:, :], glu_s.reshape(nb_s, dec_seq, D)], axis=1)
    return (y_p, y_s.reshape(nb_s, dec_seq, D), conv_p[None], new_conv_s[None],
            cv_p[None], cv_s.reshape(1, nb_s, dec_seq, D))
```

```python
import functools

import jax
import jax.numpy as jnp
from jax import lax
from jax.experimental import pallas as pl
from jax.experimental.pallas import tpu as pltpu

D = 1024
CHUNK = 128
N_GROUPS = 4
GROUP = D // N_GROUPS
CONV_W = 31
HALO = CONV_W - 1
HALO_PAD = 32
LANES = 128
LANE_TILES = D // LANES
D_FF = 2816
EPS = 1e-6
TM = 256
SUB = CHUNK
N_SUB = TM // SUB
CHAIN_LAG = 1
SUBLANES = 8
FFN_CHUNKS = ((0, 1024), (1024, 1024), (2048, 768))
HALO_TILES = HALO_PAD // SUBLANES
VMEM_LIMIT_BYTES = 62 * 1024 * 1024

F32 = jnp.float32
BF16 = jnp.bfloat16


def _rms(x, g):
    ms = jnp.mean(x * x, axis=-1, keepdims=True)
    return x * lax.rsqrt(ms + EPS) * g


def _ln(x, g, b):
    mu = jnp.mean(x, axis=-1, keepdims=True)
    xc = x - mu
    var = jnp.mean(xc * xc, axis=-1, keepdims=True)
    return xc * lax.rsqrt(var + EPS) * g + b


def _dot(a, b):
    return jnp.dot(a, b, preferred_element_type=F32)


def _mod_kernel(c_ref, w_ref, b_ref, o_ref):
    c = c_ref[...]
    s = (c * jax.nn.sigmoid(c)).astype(BF16)
    o_ref[...] = _dot(s, w_ref[...].astype(BF16)) + b_ref[...]


def _drive(chains, lag):
    live = list(chains)
    step = 0
    while live:
        for c in chains[:step // lag + 1]:
            if c in live:
                try:
                    next(c)
                except StopIteration:
                    live.remove(c)
        step += 1


def _layer_kernel(sample, *refs):
    if sample:
        (x_ref, mod_ref, cache_ref, w1_ref, w2_ref, *rest) = refs
    else:
        (x_ref, mod_ref, cwt_ref, cbt_ref, *rest) = refs
    (g1_ref, w_in_ref, b_in_ref, lnv_g_ref, lnv_b_ref, wmix_ref, bmix_ref,
     w_pa_ref, b_pa_ref, conv_b_ref, lnc_g_ref, lnc_b_ref, w_pb_ref, b_pb_ref,
     w_o_ref, g2_ref, w_fi_ref, w_fo_ref, gf_ref, *rest) = rest
    if sample:
        y_ref, glu_out_ref, cv_out_ref = rest
    else:
        y_ref, conv_out_ref, cv_out_ref, *scratch = rest
        ext_refs, cvz_refs = scratch[:N_SUB], scratch[N_SUB:]

        @pl.when(pl.program_id(1) == 0)
        def _():
            ext_refs[0][0:HALO_TILES] = jnp.zeros((HALO_TILES, SUBLANES * LANE_TILES, LANES), F32)

    nt = SUB // SUBLANES

    def row_idx(tile, j):
        return (tile, pl.ds(j, LANE_TILES, stride=SUBLANES), slice(None))

    def lane_tiles_to_rows(ref, lo, n):
        return jnp.concatenate(
            [ref[lo:lo + n, SUBLANES * s:SUBLANES * (s + 1), :].reshape(n * SUBLANES, LANES)
             for s in range(LANE_TILES)], axis=1)

    def rows_to_lane_tiles(ref, lo, val):
        n = val.shape[0] // SUBLANES
        for s in range(LANE_TILES):
            ref[lo:lo + n, SUBLANES * s:SUBLANES * (s + 1), :] = (
                val[:, LANES * s:LANES * (s + 1)].reshape(n, SUBLANES, LANES))

    def chain(h):
        r0 = h * SUB
        x = x_ref[r0:r0 + SUB, :]
        if sample:
            def mod_row(i):
                r = mod_ref[h * nt:(h + 1) * nt, i:i + 1, :]
                return jnp.broadcast_to(r, (nt, SUBLANES, D)).reshape(SUB, D)
        else:
            def mod_row(i):
                return mod_ref[i:i + 1, :]
        sh1, sc1, gt1, sh2, sc2, gt2 = [mod_row(i) for i in range(6)]
        hb = (_rms(x, g1_ref[...]) * (1.0 + sc1) + sh1).astype(BF16)
        yield

        p = [_dot(hb, w_in_ref[:, k * D:(k + 1) * D]) + b_in_ref[:, k * D:(k + 1) * D]
             for k in range(6)]
        yield

        u = jax.nn.gelu(p[0])
        v = _ln(jax.nn.gelu(p[1]), lnv_g_ref[...], lnv_b_ref[...])
        if sample:
            cv_out_ref[r0:r0 + SUB, :] = v
        elif h == N_SUB - 1:
            cv_out_ref[...] = v
        vb = v.astype(BF16)
        yield

        mixed = jnp.concatenate(
            [_dot(wmix_ref[g], vb[:, g * GROUP:(g + 1) * GROUP]) for g in range(N_GROUPS)],
            axis=1) + bmix_ref[...]
        yield

        ua = (u * mixed).astype(BF16)
        glu = p[2] * jax.nn.sigmoid(p[3])
        if sample:
            glu_out_ref[r0:r0 + SUB, :] = glu
            glu3 = glu.reshape(nt, SUBLANES, D)
            acc = jnp.broadcast_to(conv_b_ref[...], (SUBLANES, D))[None]
            for j in range(HALO):
                acc = acc + cache_ref[h * nt:(h + 1) * nt, j:j + 1, :] * w1_ref[j][None]
            for s in range(SUBLANES):
                acc = acc + glu3[:, s:s + 1, :] * w2_ref[s][None]
            cv = acc.reshape(SUB, D)
        else:
            ext, cvz = ext_refs[h], cvz_refs[h]
            rows_to_lane_tiles(ext, HALO_TILES, glu)
            rows_to_lane_tiles(ext_refs[(h + 1) % N_SUB], 0, glu[SUB - HALO_PAD:, :])
            if h == N_SUB - 1:
                conv_out_ref[...] = glu[SUB - HALO:, :]
            for i in range(nt):
                accs = [cbt_ref[...]] * SUBLANES
                for m in range(HALO_PAD - HALO, HALO_PAD + SUBLANES):
                    a_, j = divmod(m, SUBLANES)
                    z = ext[row_idx(i + a_, j)]
                    for t in range(SUBLANES):
                        k = m - t - (HALO_PAD - HALO)
                        if 0 <= k < CONV_W:
                            accs[t] = accs[t] + z * cwt_ref[SUBLANES * k:SUBLANES * (k + 1), :]
                for t in range(SUBLANES):
                    cvz[row_idx(i, t)] = accs[t]
            cv = lane_tiles_to_rows(cvz, 0, nt)
        cn = _ln(cv, lnc_g_ref[...], lnc_b_ref[...])
        cnb = (cn * jax.nn.sigmoid(cn)).astype(BF16)
        yield

        a = _dot(ua, w_pa_ref[...]) + b_pa_ref[...]
        bb = _dot(cnb, w_pb_ref[...]) + b_pb_ref[...]
        yield

        merged = (jax.nn.sigmoid(p[4]) * a + jax.nn.sigmoid(p[5]) * bb).astype(BF16)
        yield

        x1 = x + gt1 * _dot(merged, w_o_ref[...])
        yield

        h2 = (_rms(x1, g2_ref[...]) * (1.0 + sc2) + sh2).astype(BF16)
        yield

        ffn = None
        for lo, w in FFN_CHUNKS:
            gate = _dot(h2, w_fi_ref[:, lo:lo + w])
            up = _dot(h2, w_fi_ref[:, D_FF + lo:D_FF + lo + w])
            yield
            act = (gate * jax.nn.sigmoid(gate) * up).astype(BF16)
            yield
            part = _dot(act, w_fo_ref[lo:lo + w, :])
            ffn = part if ffn is None else ffn + part
        yield

        x2 = x1 + gt2 * ffn
        y_ref[r0:r0 + SUB, :] = _rms(x2, gf_ref[...])

    _drive([chain(h) for h in range(N_SUB)], CHAIN_LAG)


def _resident(shape):
    nd = len(shape)
    return pl.BlockSpec(shape, lambda *_: (0,) * nd, pipeline_mode=pl.Buffered(1))


def _modulation(c_all, w_ada, b_ada):
    n = c_all.shape[0]
    nb = 4
    bw = 6 * D // nb
    return pl.pallas_call(
        _mod_kernel,
        grid=(nb,),
        in_specs=[
            pl.BlockSpec((n, D), lambda i: (0, 0)),
            pl.BlockSpec((D, bw), lambda i: (0, i)),
            pl.BlockSpec((1, bw), lambda i: (0, i)),
        ],
        out_specs=pl.BlockSpec((n, bw), lambda i: (0, i)),
        out_shape=jax.ShapeDtypeStruct((n, 6 * D), F32),
        name="adaln_mod",
    )(c_all, w_ada, b_ada)


def kernel(x_prompt, x_sample, c_prompt, c_sample, cache_conv, w_ada, b_ada, g_norm1,
           w_in, b_in, ln_v_g, ln_v_b, w_s, b_s, w_pa, b_pa, conv_w, conv_b,
           ln_c_g, ln_c_b, w_pb, b_pb, w_o, g_norm2, w_ffn_in, w_ffn_out, g_final):
    nb_p, seq, _ = x_prompt.shape
    nb_s, dec_seq, _ = x_sample.shape
    assert dec_seq == SUBLANES and seq % TM == 0 and (nb_s * dec_seq) % TM == 0
    assert w_ada.shape[0] == 1

    mod = _modulation(jnp.concatenate([c_prompt, c_sample], axis=0), w_ada[0],
                      b_ada[0][None]).reshape(nb_p + nb_s, 6, D)
    mod_p, mod_s = mod[:nb_p], mod[nb_p:]

    row = lambda a: a[0][None].astype(F32)
    tril = jnp.tril(jnp.ones((CHUNK, CHUNK), bool))
    wmix_p = jnp.where(tril[None], w_s[0], 0.0).astype(BF16)
    bmix_p = jnp.repeat(b_s[0].T, GROUP, axis=1)
    reps = CHUNK // dec_seq
    tril_s = jnp.tril(jnp.ones((dec_seq, dec_seq), bool))
    w_small = jnp.where(tril_s[None], w_s[0][:, :dec_seq, :dec_seq], 0.0)
    wmix_s = jax.vmap(lambda m: jnp.kron(jnp.eye(reps, dtype=F32), m))(w_small).astype(BF16)
    bmix_s = jnp.tile(jnp.repeat(b_s[0][:, :dec_seq].T, GROUP, axis=1), (reps, 1))
    cw = conv_w[0]
    cwt = cw.reshape(CONV_W * LANE_TILES, LANES)
    cbt = conv_b[0].reshape(LANE_TILES, LANES)
    t_idx = jnp.arange(dec_seq)
    j_idx = jnp.arange(HALO)
    k1 = j_idx[:, None] - t_idx[None, :]
    w1 = jnp.where((k1 >= 0)[..., None], cw[jnp.clip(k1, 0, CONV_W - 1)], 0.0)
    k2 = HALO - t_idx[None, :] + t_idx[:, None]
    w2 = jnp.where((k2 <= HALO)[..., None], cw[jnp.clip(k2, 0, CONV_W - 1)], 0.0)

    def weights(wmix, bmix):
        return [
            row(g_norm1), w_in[0].astype(BF16), row(b_in), row(ln_v_g), row(ln_v_b),
            wmix, bmix, w_pa[0].astype(BF16), row(b_pa), row(conv_b), row(ln_c_g),
            row(ln_c_b), w_pb[0].astype(BF16), row(b_pb), w_o[0].astype(BF16),
            row(g_norm2), w_ffn_in[0].astype(BF16), w_ffn_out[0].astype(BF16),
            g_final[None].astype(F32),
        ]

    params = pltpu.CompilerParams(
        dimension_semantics=("arbitrary", "arbitrary"),
        vmem_limit_bytes=VMEM_LIMIT_BYTES)

    wl = weights(wmix_p, bmix_p)
    nt = seq // TM
    y_p, conv_p, cv_p = pl.pallas_call(
        functools.partial(_layer_kernel, False),
        grid=(nb_p, nt),
        in_specs=[
            pl.BlockSpec((None, TM, D), lambda b, j: (b, j, 0)),
            pl.BlockSpec((None, 6, D), lambda b, j: (b, 0, 0)),
            _resident(cwt.shape), _resident(cbt.shape),
        ] + [_resident(w.shape) for w in wl],
        out_specs=[
            pl.BlockSpec((None, TM, D), lambda b, j: (b, j, 0)),
            pl.BlockSpec((None, HALO, D), lambda b, j: (b, 0, 0)),
            pl.BlockSpec((None, CHUNK, D), lambda b, j: (b, 0, 0)),
        ],
        out_shape=[
            jax.ShapeDtypeStruct((nb_p, seq, D), F32),
            jax.ShapeDtypeStruct((nb_p, HALO, D), F32),
            jax.ShapeDtypeStruct((nb_p, CHUNK, D), F32),
        ],
        scratch_shapes=(
            [pltpu.VMEM((HALO_TILES + SUB // SUBLANES, SUBLANES * LANE_TILES, LANES), F32)] * N_SUB
            + [pltpu.VMEM((SUB // SUBLANES, SUBLANES * LANE_TILES, LANES), F32)] * N_SUB),
        compiler_params=params,
        name="layer_prompt",
    )(x_prompt, mod_p, cwt, cbt, *wl)

    wl = weights(wmix_s, bmix_s)
    rows_s = nb_s * dec_seq
    ns = TM // dec_seq
    y_s, glu_s, cv_s = pl.pallas_call(
        functools.partial(_layer_kernel, True),
        grid=(1, rows_s // TM),
        in_specs=[
            pl.BlockSpec((TM, D), lambda b, j: (j, 0)),
            pl.BlockSpec((ns, 6, D), lambda b, j: (j, 0, 0)),
            pl.BlockSpec((ns, HALO, D), lambda b, j: (j, 0, 0)),
            _resident(w1.shape), _resident(w2.shape),
        ] + [_resident(w.shape) for w in wl],
        out_specs=[
            pl.BlockSpec((TM, D), lambda b, j: (j, 0)),
            pl.BlockSpec((TM, D), lambda b, j: (j, 0)),
            pl.BlockSpec((TM, D), lambda b, j: (j, 0)),
        ],
        out_shape=[jax.ShapeDtypeStruct((rows_s, D), F32)] * 3,
        compiler_params=params,
        name="layer_sample",
    )(x_sample.reshape(rows_s, D), mod_s, cache_conv[0], w1, w2, *wl)

    new_conv_s = jnp.concatenate(
        [cache_conv[0][:, dec_seq:, :], glu_s.reshape(nb_s, dec_seq, D)], axis=1)
    return (y_p, y_s.reshape(nb_s, dec_seq, D), conv_p[None], new_conv_s[None],
            cv_p[None], cv_s.reshape(1, nb_s, dec_seq, D))
```

```python
import functools

import jax
import jax.numpy as jnp
from jax import lax
from jax.experimental import pallas as pl
from jax.experimental.pallas import tpu as pltpu

D = 1024
CHUNK = 128
N_GROUPS = 4
GROUP = D // N_GROUPS
CONV_W = 31
HALO = CONV_W - 1
HALO_PAD = 32
LANES = 128
LANE_TILES = D // LANES
D_FF = 2816
EPS = 1e-6
TM = 256
SUB = CHUNK
N_SUB = TM // SUB
CHAIN_LAG = 1
SUBLANES = 8
FFN_CHUNKS = ((0, 1024), (1024, 1024), (2048, 768))
HALO_TILES = HALO_PAD // SUBLANES
VMEM_LIMIT_BYTES = 62 * 1024 * 1024

F32 = jnp.float32
BF16 = jnp.bfloat16
U32 = jnp.uint32


def _rms(x, g):
    ms = jnp.mean(x * x, axis=-1, keepdims=True)
    return x * lax.rsqrt(ms + EPS) * g


def _ln(x, g, b):
    mu = jnp.mean(x, axis=-1, keepdims=True)
    xc = x - mu
    var = jnp.mean(xc * xc, axis=-1, keepdims=True)
    return xc * lax.rsqrt(var + EPS) * g + b


def _dot(a, b):
    return jnp.dot(a, b, preferred_element_type=F32)


def _pack_rows(w):
    k, n = w.shape
    pairs = jnp.swapaxes(w.astype(BF16).reshape(k // 2, 2, n), 1, 2)
    return lax.bitcast_convert_type(pairs, U32)


def _unpacked(w_ref, rows, cols):
    return pltpu.bitcast(w_ref[rows.start // 2:rows.stop // 2, cols], BF16)


def _mod_kernel(c_ref, w_ref, b_ref, o_ref):
    c = c_ref[...]
    s = (c * jax.nn.sigmoid(c)).astype(BF16)
    o_ref[...] = _dot(s, w_ref[...].astype(BF16)) + b_ref[...]


def _drive(chains, lag):
    live = list(chains)
    step = 0
    while live:
        for c in chains[:step // lag + 1]:
            if c in live:
                try:
                    next(c)
                except StopIteration:
                    live.remove(c)
        step += 1


def _layer_kernel(sample, *refs):
    if sample:
        (x_ref, mod_ref, cache_ref, w1_ref, w2_ref, *rest) = refs
    else:
        (x_ref, mod_ref, cwt_ref, cbt_ref, *rest) = refs
    (g1_ref, w_in_ref, b_in_ref, lnv_g_ref, lnv_b_ref, wmix_ref, bmix_ref,
     w_pa_ref, b_pa_ref, conv_b_ref, lnc_g_ref, lnc_b_ref, w_pb_ref, b_pb_ref,
     w_o_ref, g2_ref, w_fi_ref, w_fo_ref, gf_ref, *rest) = rest
    if sample:
        y_ref, conv_out_ref, cv_out_ref = rest
    else:
        y_ref, conv_out_ref, cv_out_ref, *scratch = rest
        ext_refs, cvz_refs = scratch[:N_SUB], scratch[N_SUB:]

        @pl.when(pl.program_id(1) == 0)
        def _():
            ext_refs[0][0:HALO_TILES] = jnp.zeros((HALO_TILES, SUBLANES * LANE_TILES, LANES), F32)

    nt = SUB // SUBLANES
    all_rows = slice(0, D)

    def row_idx(tile, j):
        return (tile, pl.ds(j, LANE_TILES, stride=SUBLANES), slice(None))

    def lane_tiles_to_rows(ref, lo, n):
        return jnp.concatenate(
            [ref[lo:lo + n, SUBLANES * s:SUBLANES * (s + 1), :].reshape(n * SUBLANES, LANES)
             for s in range(LANE_TILES)], axis=1)

    def rows_to_lane_tiles(ref, lo, val):
        n = val.shape[0] // SUBLANES
        for s in range(LANE_TILES):
            ref[lo:lo + n, SUBLANES * s:SUBLANES * (s + 1), :] = (
                val[:, LANES * s:LANES * (s + 1)].reshape(n, SUBLANES, LANES))

    def chain(h):
        r0 = h * SUB
        x = x_ref[r0:r0 + SUB, :]
        if sample:
            def mod_row(i):
                r = mod_ref[h * nt:(h + 1) * nt, i:i + 1, :]
                return jnp.broadcast_to(r, (nt, SUBLANES, D)).reshape(SUB, D)
        else:
            def mod_row(i):
                return mod_ref[i:i + 1, :]
        sh1, sc1, gt1, sh2, sc2, gt2 = [mod_row(i) for i in range(6)]
        hb = (_rms(x, g1_ref[...]) * (1.0 + sc1) + sh1).astype(BF16)
        yield

        p = [_dot(hb, _unpacked(w_in_ref, all_rows, slice(k * D, (k + 1) * D)))
             + b_in_ref[:, k * D:(k + 1) * D] for k in range(6)]
        yield

        u = jax.nn.gelu(p[0])
        v = _ln(jax.nn.gelu(p[1]), lnv_g_ref[...], lnv_b_ref[...])
        if sample:
            cv_out_ref[r0:r0 + SUB, :] = v
        elif h == N_SUB - 1:
            cv_out_ref[...] = v
        vb = v.astype(BF16)
        yield

        mixed = jnp.concatenate(
            [_dot(wmix_ref[g], vb[:, g * GROUP:(g + 1) * GROUP]) for g in range(N_GROUPS)],
            axis=1) + bmix_ref[...]
        yield

        ua = (u * mixed).astype(BF16)
        glu = p[2] * jax.nn.sigmoid(p[3])
        if sample:
            seqs = slice(h * nt, (h + 1) * nt)
            glu3 = glu.reshape(nt, SUBLANES, D)
            conv_out_ref[seqs, 0:HALO - SUBLANES, :] = cache_ref[seqs, SUBLANES:HALO, :]
            conv_out_ref[seqs, HALO - SUBLANES:HALO, :] = glu3
            acc = jnp.broadcast_to(conv_b_ref[...], (SUBLANES, D))[None]
            for j in range(HALO):
                acc = acc + cache_ref[seqs, j:j + 1, :] * w1_ref[j][None]
            for s in range(SUBLANES):
                acc = acc + glu3[:, s:s + 1, :] * w2_ref[s][None]
            cv = acc.reshape(SUB, D)
        else:
            ext, cvz = ext_refs[h], cvz_refs[h]
            rows_to_lane_tiles(ext, HALO_TILES, glu)
            rows_to_lane_tiles(ext_refs[(h + 1) % N_SUB], 0, glu[SUB - HALO_PAD:, :])
            if h == N_SUB - 1:
                conv_out_ref[...] = glu[SUB - HALO:, :]
            for i in range(nt):
                accs = [cbt_ref[...]] * SUBLANES
                for m in range(HALO_PAD - HALO, HALO_PAD + SUBLANES):
                    a_, j = divmod(m, SUBLANES)
                    z = ext[row_idx(i + a_, j)]
                    for r in range(SUBLANES):
                        k = m - r - (HALO_PAD - HALO)
                        if 0 <= k < CONV_W:
                            accs[r] = accs[r] + z * cwt_ref[SUBLANES * k:SUBLANES * (k + 1), :]
                for r in range(SUBLANES):
                    cvz[row_idx(i, r)] = accs[r]
            cv = lane_tiles_to_rows(cvz, 0, nt)
        cn = _ln(cv, lnc_g_ref[...], lnc_b_ref[...])
        cnb = (cn * jax.nn.sigmoid(cn)).astype(BF16)
        yield

        sq = slice(0, D)
        a = _dot(ua, _unpacked(w_pa_ref, all_rows, sq)) + b_pa_ref[...]
        bb = _dot(cnb, _unpacked(w_pb_ref, all_rows, sq)) + b_pb_ref[...]
        yield

        merged = (jax.nn.sigmoid(p[4]) * a + jax.nn.sigmoid(p[5]) * bb).astype(BF16)
        yield

        x1 = x + gt1 * _dot(merged, _unpacked(w_o_ref, all_rows, sq))
        yield

        h2 = (_rms(x1, g2_ref[...]) * (1.0 + sc2) + sh2).astype(BF16)
        yield

        ffn = None
        for lo, w in FFN_CHUNKS:
            gate = _dot(h2, _unpacked(w_fi_ref, all_rows, slice(lo, lo + w)))
            up = _dot(h2, _unpacked(w_fi_ref, all_rows, slice(D_FF + lo, D_FF + lo + w)))
            yield
            act = (gate * jax.nn.sigmoid(gate) * up).astype(BF16)
            yield
            part = _dot(act, _unpacked(w_fo_ref, slice(lo, lo + w), sq))
            ffn = part if ffn is None else ffn + part
        yield

        x2 = x1 + gt2 * ffn
        y_ref[r0:r0 + SUB, :] = _rms(x2, gf_ref[...])

    _drive([chain(h) for h in range(N_SUB)], CHAIN_LAG)


def _resident(shape):
    nd = len(shape)
    return pl.BlockSpec(shape, lambda *_: (0,) * nd, pipeline_mode=pl.Buffered(1))


def _modulation(c_all, w_ada, b_ada):
    n = c_all.shape[0]
    nb = 4
    bw = 6 * D // nb
    return pl.pallas_call(
        _mod_kernel,
        grid=(nb,),
        in_specs=[
            pl.BlockSpec((n, D), lambda i: (0, 0)),
            pl.BlockSpec((D, bw), lambda i: (0, i)),
            pl.BlockSpec((1, bw), lambda i: (0, i)),
        ],
        out_specs=pl.BlockSpec((n, bw), lambda i: (0, i)),
        out_shape=jax.ShapeDtypeStruct((n, 6 * D), F32),
        name="adaln_mod",
    )(c_all, w_ada, b_ada)


def kernel(x_prompt, x_sample, c_prompt, c_sample, cache_conv, w_ada, b_ada, g_norm1,
           w_in, b_in, ln_v_g, ln_v_b, w_s, b_s, w_pa, b_pa, conv_w, conv_b,
           ln_c_g, ln_c_b, w_pb, b_pb, w_o, g_norm2, w_ffn_in, w_ffn_out, g_final):
    nb_p, seq, _ = x_prompt.shape
    nb_s, dec_seq, _ = x_sample.shape
    assert dec_seq == SUBLANES and seq % TM == 0 and (nb_s * dec_seq) % TM == 0
    assert w_ada.shape[0] == 1

    mod = _modulation(jnp.concatenate([c_prompt, c_sample], axis=0), w_ada[0],
                      b_ada[0][None]).reshape(nb_p + nb_s, 6, D)
    mod_p, mod_s = mod[:nb_p], mod[nb_p:]

    row = lambda a: a[0][None].astype(F32)
    tril = jnp.tril(jnp.ones((CHUNK, CHUNK), bool))
    wmix_p = jnp.where(tril[None], w_s[0], 0.0).astype(BF16)
    bmix_p = jnp.repeat(b_s[0].T, GROUP, axis=1)
    reps = CHUNK // dec_seq
    tril_s = jnp.tril(jnp.ones((dec_seq, dec_seq), bool))
    w_small = jnp.where(tril_s[None], w_s[0][:, :dec_seq, :dec_seq], 0.0)
    wmix_s = jax.vmap(lambda m: jnp.kron(jnp.eye(reps, dtype=F32), m))(w_small).astype(BF16)
    bmix_s = jnp.tile(jnp.repeat(b_s[0][:, :dec_seq].T, GROUP, axis=1), (reps, 1))
    cw = conv_w[0]
    cwt = cw.reshape(CONV_W * LANE_TILES, LANES)
    cbt = conv_b[0].reshape(LANE_TILES, LANES)
    w1 = jnp.stack([jnp.pad(cw[:HALO - r], ((r, 0), (0, 0))) for r in range(dec_seq)], axis=1)
    w2 = jnp.stack([jnp.pad(cw[HALO - r:], ((0, dec_seq - 1 - r), (0, 0))) for r in range(dec_seq)],
                   axis=1)

    shared = [
        row(g_norm1), _pack_rows(w_in[0]), row(b_in), row(ln_v_g), row(ln_v_b),
        None, None, _pack_rows(w_pa[0]), row(b_pa), row(conv_b), row(ln_c_g),
        row(ln_c_b), _pack_rows(w_pb[0]), row(b_pb), _pack_rows(w_o[0]),
        row(g_norm2), _pack_rows(w_ffn_in[0]), _pack_rows(w_ffn_out[0]),
        g_final[None].astype(F32),
    ]

    def weights(wmix, bmix):
        return shared[:5] + [wmix, bmix] + shared[7:]

    params = pltpu.CompilerParams(
        dimension_semantics=("arbitrary", "arbitrary"),
        vmem_limit_bytes=VMEM_LIMIT_BYTES)

    wp = weights(wmix_p, bmix_p)
    lane_tile_shape = (SUBLANES * LANE_TILES, LANES)
    y_p, conv_p, cv_p = pl.pallas_call(
        functools.partial(_layer_kernel, False),
        grid=(nb_p, seq // TM),
        in_specs=[
            pl.BlockSpec((None, TM, D), lambda b, j: (b, j, 0)),
            pl.BlockSpec((None, 6, D), lambda b, j: (b, 0, 0)),
            _resident(cwt.shape), _resident(cbt.shape),
        ] + [_resident(w.shape) for w in wp],
        out_specs=[
            pl.BlockSpec((None, TM, D), lambda b, j: (b, j, 0)),
            pl.BlockSpec((None, HALO, D), lambda b, j: (b, 0, 0)),
            pl.BlockSpec((None, CHUNK, D), lambda b, j: (b, 0, 0)),
        ],
        out_shape=[
            jax.ShapeDtypeStruct((nb_p, seq, D), F32),
            jax.ShapeDtypeStruct((nb_p, HALO, D), F32),
            jax.ShapeDtypeStruct((nb_p, CHUNK, D), F32),
        ],
        scratch_shapes=(
            [pltpu.VMEM((HALO_TILES + SUB // SUBLANES,) + lane_tile_shape, F32)] * N_SUB
            + [pltpu.VMEM((SUB // SUBLANES,) + lane_tile_shape, F32)] * N_SUB),
        compiler_params=params,
        name="layer_prompt",
    )(x_prompt, mod_p, cwt, cbt, *wp)

    ws = weights(wmix_s, bmix_s)
    rows_s = nb_s * dec_seq
    ns = TM // dec_seq
    once = pl.Buffered(1)
    y_s, conv_s, cv_s = pl.pallas_call(
        functools.partial(_layer_kernel, True),
        grid=(1, rows_s // TM),
        in_specs=[
            pl.BlockSpec((TM, D), lambda b, j: (j, 0)),
            pl.BlockSpec((ns, 6, D), lambda b, j: (j, 0, 0)),
            pl.BlockSpec((ns, HALO, D), lambda b, j: (j, 0, 0), pipeline_mode=once),
            _resident(w1.shape), _resident(w2.shape),
        ] + [_resident(w.shape) for w in ws],
        out_specs=[
            pl.BlockSpec((TM, D), lambda b, j: (j, 0)),
            pl.BlockSpec((ns, HALO, D), lambda b, j: (j, 0, 0), pipeline_mode=once),
            pl.BlockSpec((TM, D), lambda b, j: (j, 0)),
        ],
        out_shape=[
            jax.ShapeDtypeStruct((rows_s, D), F32),
            jax.ShapeDtypeStruct((nb_s, HALO, D), F32),
            jax.ShapeDtypeStruct((rows_s, D), F32),
        ],
        compiler_params=params,
        name="layer_sample",
    )(x_sample.reshape(rows_s, D), mod_s, cache_conv[0], w1, w2, *ws)

    return (y_p, y_s.reshape(nb_s, dec_seq, D), conv_p[None], conv_s[None],
            cv_p[None], cv_s.reshape(1, nb_s, dec_seq, D))
```

```python
import functools

import jax
import jax.numpy as jnp
from jax import lax
from jax.experimental import pallas as pl
from jax.experimental.pallas import tpu as pltpu

D = 1024
CHUNK = 128
N_GROUPS = 4
GROUP = D // N_GROUPS
CONV_W = 31
HALO = CONV_W - 1
HALO_PAD = 32
LANES = 128
LANE_TILES = D // LANES
D_FF = 2816
EPS = 1e-6
TM = 256
SUB = CHUNK
N_SUB = TM // SUB
CHAIN_LAG = 2
SUBLANES = 8
FFN_CHUNKS = ((0, 1024), (1024, 1024), (2048, 768))
HALO_TILES = HALO_PAD // SUBLANES
VMEM_LIMIT_BYTES = 62 * 1024 * 1024

F32 = jnp.float32
BF16 = jnp.bfloat16


def _rms(x, g):
    ms = jnp.mean(x * x, axis=-1, keepdims=True)
    return x * lax.rsqrt(ms + EPS) * g


def _ln(x, g, b):
    mu = jnp.mean(x, axis=-1, keepdims=True)
    xc = x - mu
    var = jnp.mean(xc * xc, axis=-1, keepdims=True)
    return xc * lax.rsqrt(var + EPS) * g + b


def _dot(a, b):
    return jnp.dot(a, b, preferred_element_type=F32)


def _pack_rows(w):
    return w.astype(BF16)


def _unpacked(w_ref, rows, cols):
    return w_ref[rows, cols]


def _mod_kernel(c_ref, w_ref, b_ref, o_ref):
    c = c_ref[...]
    s = (c * jax.nn.sigmoid(c)).astype(BF16)
    o_ref[...] = _dot(s, w_ref[...].astype(BF16)) + b_ref[...]


def _drive(chains, lag):
    live = list(chains)
    step = 0
    while live:
        for c in chains[:step // lag + 1]:
            if c in live:
                try:
                    next(c)
                except StopIteration:
                    live.remove(c)
        step += 1


def _layer_kernel(sample, *refs):
    if sample:
        (x_ref, mod_ref, cache_ref, w1_ref, w2_ref, *rest) = refs
    else:
        (x_ref, mod_ref, cwt_ref, cbt_ref, *rest) = refs
    (g1_ref, w_in_ref, b_in_ref, lnv_g_ref, lnv_b_ref, wmix_ref, bmix_ref,
     w_pa_ref, b_pa_ref, conv_b_ref, lnc_g_ref, lnc_b_ref, w_pb_ref, b_pb_ref,
     w_o_ref, g2_ref, w_fi_ref, w_fo_ref, gf_ref, *rest) = rest
    if sample:
        y_ref, conv_out_ref, cv_out_ref = rest
    else:
        y_ref, conv_out_ref, cv_out_ref, *scratch = rest
        ext_refs, cvz_refs = scratch[:N_SUB], scratch[N_SUB:]

        @pl.when(pl.program_id(1) == 0)
        def _():
            ext_refs[0][0:HALO_TILES] = jnp.zeros((HALO_TILES, SUBLANES * LANE_TILES, LANES), F32)

    nt = SUB // SUBLANES
    all_rows = slice(0, D)

    def row_idx(tile, j):
        return (tile, pl.ds(j, LANE_TILES, stride=SUBLANES), slice(None))

    def lane_tiles_to_rows(ref, lo, n):
        return jnp.concatenate(
            [ref[lo:lo + n, SUBLANES * s:SUBLANES * (s + 1), :].reshape(n * SUBLANES, LANES)
             for s in range(LANE_TILES)], axis=1)

    def rows_to_lane_tiles(ref, lo, val):
        n = val.shape[0] // SUBLANES
        for s in range(LANE_TILES):
            ref[lo:lo + n, SUBLANES * s:SUBLANES * (s + 1), :] = (
                val[:, LANES * s:LANES * (s + 1)].reshape(n, SUBLANES, LANES))

    def chain(h):
        r0 = h * SUB
        x = x_ref[r0:r0 + SUB, :]
        if sample:
            def mod_row(i):
                r = mod_ref[h * nt:(h + 1) * nt, i:i + 1, :]
                return jnp.broadcast_to(r, (nt, SUBLANES, D)).reshape(SUB, D)
        else:
            def mod_row(i):
                return mod_ref[i:i + 1, :]
        sh1, sc1, gt1, sh2, sc2, gt2 = [mod_row(i) for i in range(6)]
        hb = (_rms(x, g1_ref[...]) * (1.0 + sc1) + sh1).astype(BF16)
        yield

        p = [_dot(hb, _unpacked(w_in_ref, all_rows, slice(k * D, (k + 1) * D)))
             + b_in_ref[:, k * D:(k + 1) * D] for k in range(6)]
        yield

        u = jax.nn.gelu(p[0])
        v = _ln(jax.nn.gelu(p[1]), lnv_g_ref[...], lnv_b_ref[...])
        if sample:
            cv_out_ref[r0:r0 + SUB, :] = v
        elif h == N_SUB - 1:
            cv_out_ref[...] = v
        vb = v.astype(BF16)
        yield

        mixed = jnp.concatenate(
            [_dot(wmix_ref[g], vb[:, g * GROUP:(g + 1) * GROUP]) for g in range(N_GROUPS)],
            axis=1) + bmix_ref[...]
        yield

        ua = (u * mixed).astype(BF16)
        glu = p[2] * jax.nn.sigmoid(p[3])
        if sample:
            seqs = slice(h * nt, (h + 1) * nt)
            glu3 = glu.reshape(nt, SUBLANES, D)
            conv_out_ref[seqs, 0:HALO - SUBLANES, :] = cache_ref[seqs, SUBLANES:HALO, :]
            conv_out_ref[seqs, HALO - SUBLANES:HALO, :] = glu3
            acc = jnp.broadcast_to(conv_b_ref[...], (SUBLANES, D))[None]
            for j in range(HALO):
                acc = acc + cache_ref[seqs, j:j + 1, :] * w1_ref[j][None]
            for s in range(SUBLANES):
                acc = acc + glu3[:, s:s + 1, :] * w2_ref[s][None]
            cv = acc.reshape(SUB, D)
        else:
            ext, cvz = ext_refs[h], cvz_refs[h]
            rows_to_lane_tiles(ext, HALO_TILES, glu)
            rows_to_lane_tiles(ext_refs[(h + 1) % N_SUB], 0, glu[SUB - HALO_PAD:, :])
            if h == N_SUB - 1:
                conv_out_ref[...] = glu[SUB - HALO:, :]
            for i in range(nt):
                accs = [cbt_ref[...]] * SUBLANES
                for m in range(HALO_PAD - HALO, HALO_PAD + SUBLANES):
                    a_, j = divmod(m, SUBLANES)
                    z = ext[row_idx(i + a_, j)]
                    for r in range(SUBLANES):
                        k = m - r - (HALO_PAD - HALO)
                        if 0 <= k < CONV_W:
                            accs[r] = accs[r] + z * cwt_ref[SUBLANES * k:SUBLANES * (k + 1), :]
                for r in range(SUBLANES):
                    cvz[row_idx(i, r)] = accs[r]
            cv = lane_tiles_to_rows(cvz, 0, nt)
        cn = _ln(cv, lnc_g_ref[...], lnc_b_ref[...])
        cnb = (cn * jax.nn.sigmoid(cn)).astype(BF16)
        yield

        sq = slice(0, D)
        a = _dot(ua, _unpacked(w_pa_ref, all_rows, sq)) + b_pa_ref[...]
        bb = _dot(cnb, _unpacked(w_pb_ref, all_rows, sq)) + b_pb_ref[...]
        yield

        merged = (jax.nn.sigmoid(p[4]) * a + jax.nn.sigmoid(p[5]) * bb).astype(BF16)
        yield

        x1 = x + gt1 * _dot(merged, _unpacked(w_o_ref, all_rows, sq))
        yield

        h2 = (_rms(x1, g2_ref[...]) * (1.0 + sc2) + sh2).astype(BF16)
        yield

        ffn = None
        for lo, w in FFN_CHUNKS:
            gate = _dot(h2, _unpacked(w_fi_ref, all_rows, slice(lo, lo + w)))
            up = _dot(h2, _unpacked(w_fi_ref, all_rows, slice(D_FF + lo, D_FF + lo + w)))
            yield
            act = (gate * jax.nn.sigmoid(gate) * up).astype(BF16)
            yield
            part = _dot(act, _unpacked(w_fo_ref, slice(lo, lo + w), sq))
            ffn = part if ffn is None else ffn + part
        yield

        x2 = x1 + gt2 * ffn
        y_ref[r0:r0 + SUB, :] = _rms(x2, gf_ref[...])

    _drive([chain(h) for h in range(N_SUB)], CHAIN_LAG)


def _resident(shape):
    nd = len(shape)
    return pl.BlockSpec(shape, lambda *_: (0,) * nd, pipeline_mode=pl.Buffered(1))


def _modulation(c_all, w_ada, b_ada):
    n = c_all.shape[0]
    nb = 4
    bw = 6 * D // nb
    return pl.pallas_call(
        _mod_kernel,
        grid=(nb,),
        in_specs=[
            pl.BlockSpec((n, D), lambda i: (0, 0)),
            pl.BlockSpec((D, bw), lambda i: (0, i)),
            pl.BlockSpec((1, bw), lambda i: (0, i)),
        ],
        out_specs=pl.BlockSpec((n, bw), lambda i: (0, i)),
        out_shape=jax.ShapeDtypeStruct((n, 6 * D), F32),
        name="adaln_mod",
    )(c_all, w_ada, b_ada)


def kernel(x_prompt, x_sample, c_prompt, c_sample, cache_conv, w_ada, b_ada, g_norm1,
           w_in, b_in, ln_v_g, ln_v_b, w_s, b_s, w_pa, b_pa, conv_w, conv_b,
           ln_c_g, ln_c_b, w_pb, b_pb, w_o, g_norm2, w_ffn_in, w_ffn_out, g_final):
    nb_p, seq, _ = x_prompt.shape
    nb_s, dec_seq, _ = x_sample.shape
    assert dec_seq == SUBLANES and seq % TM == 0 and (nb_s * dec_seq) % TM == 0
    assert w_ada.shape[0] == 1

    mod = _modulation(jnp.concatenate([c_prompt, c_sample], axis=0), w_ada[0],
                      b_ada[0][None]).reshape(nb_p + nb_s, 6, D)
    mod_p, mod_s = mod[:nb_p], mod[nb_p:]

    row = lambda a: a[0][None].astype(F32)
    tril = jnp.tril(jnp.ones((CHUNK, CHUNK), bool))
    wmix_p = jnp.where(tril[None], w_s[0], 0.0).astype(BF16)
    bmix_p = jnp.repeat(b_s[0].T, GROUP, axis=1)
    reps = CHUNK // dec_seq
    tril_s = jnp.tril(jnp.ones((dec_seq, dec_seq), bool))
    w_small = jnp.where(tril_s[None], w_s[0][:, :dec_seq, :dec_seq], 0.0)
    wmix_s = jax.vmap(lambda m: jnp.kron(jnp.eye(reps, dtype=F32), m))(w_small).astype(BF16)
    bmix_s = jnp.tile(jnp.repeat(b_s[0][:, :dec_seq].T, GROUP, axis=1), (reps, 1))
    cw = conv_w[0]
    cwt = cw.reshape(CONV_W * LANE_TILES, LANES)
    cbt = conv_b[0].reshape(LANE_TILES, LANES)
    w1 = jnp.stack([jnp.pad(cw[:HALO - r], ((r, 0), (0, 0))) for r in range(dec_seq)], axis=1)
    w2 = jnp.stack([jnp.pad(cw[HALO - r:], ((0, dec_seq - 1 - r), (0, 0))) for r in range(dec_seq)],
                   axis=1)

    shared = [
        row(g_norm1), _pack_rows(w_in[0]), row(b_in), row(ln_v_g), row(ln_v_b),
        None, None, _pack_rows(w_pa[0]), row(b_pa), row(conv_b), row(ln_c_g),
        row(ln_c_b), _pack_rows(w_pb[0]), row(b_pb), _pack_rows(w_o[0]),
        row(g_norm2), _pack_rows(w_ffn_in[0]), _pack_rows(w_ffn_out[0]),
        g_final[None].astype(F32),
    ]

    def weights(wmix, bmix):
        return shared[:5] + [wmix, bmix] + shared[7:]

    params = pltpu.CompilerParams(
        dimension_semantics=("arbitrary", "arbitrary"),
        vmem_limit_bytes=VMEM_LIMIT_BYTES)

    wp = weights(wmix_p, bmix_p)
    lane_tile_shape = (SUBLANES * LANE_TILES, LANES)
    y_p, conv_p, cv_p = pl.pallas_call(
        functools.partial(_layer_kernel, False),
        grid=(nb_p, seq // TM),
        in_specs=[
            pl.BlockSpec((None, TM, D), lambda b, j: (b, j, 0)),
            pl.BlockSpec((None, 6, D), lambda b, j: (b, 0, 0)),
            _resident(cwt.shape), _resident(cbt.shape),
        ] + [_resident(w.shape) for w in wp],
        out_specs=[
            pl.BlockSpec((None, TM, D), lambda b, j: (b, j, 0)),
            pl.BlockSpec((None, HALO, D), lambda b, j: (b, 0, 0)),
            pl.BlockSpec((None, CHUNK, D), lambda b, j: (b, 0, 0)),
        ],
        out_shape=[
            jax.ShapeDtypeStruct((nb_p, seq, D), F32),
            jax.ShapeDtypeStruct((nb_p, HALO, D), F32),
            jax.ShapeDtypeStruct((nb_p, CHUNK, D), F32),
        ],
        scratch_shapes=(
            [pltpu.VMEM((HALO_TILES + SUB // SUBLANES,) + lane_tile_shape, F32)] * N_SUB
            + [pltpu.VMEM((SUB // SUBLANES,) + lane_tile_shape, F32)] * N_SUB),
        compiler_params=params,
        name="layer_prompt",
    )(x_prompt, mod_p, cwt, cbt, *wp)

    ws = weights(wmix_s, bmix_s)
    rows_s = nb_s * dec_seq
    ns = TM // dec_seq
    once = pl.Buffered(1)
    y_s, conv_s, cv_s = pl.pallas_call(
        functools.partial(_layer_kernel, True),
        grid=(1, rows_s // TM),
        in_specs=[
            pl.BlockSpec((TM, D), lambda b, j: (j, 0)),
            pl.BlockSpec((ns, 6, D), lambda b, j: (j, 0, 0)),
            pl.BlockSpec((ns, HALO, D), lambda b, j: (j, 0, 0), pipeline_mode=once),
            _resident(w1.shape), _resident(w2.shape),
        ] + [_resident(w.shape) for w in ws],
        out_specs=[
            pl.BlockSpec((TM, D), lambda b, j: (j, 0)),
            pl.BlockSpec((ns, HALO, D), lambda b, j: (j, 0, 0), pipeline_mode=once),
            pl.BlockSpec((TM, D), lambda b, j: (j, 0)),
        ],
        out_shape=[
            jax.ShapeDtypeStruct((rows_s, D), F32),
            jax.ShapeDtypeStruct((nb_s, HALO, D), F32),
            jax.ShapeDtypeStruct((rows_s, D), F32),
        ],
        compiler_params=params,
        name="layer_sample",
    )(x_sample.reshape(rows_s, D), mod_s, cache_conv[0], w1, w2, *ws)

    return (y_p, y_s.reshape(nb_s, dec_seq, D), conv_p[None], conv_s[None],
            cv_p[None], cv_s.reshape(1, nb_s, dec_seq, D))
```

```python
import functools

import jax
import jax.numpy as jnp
from jax import lax
from jax.experimental import pallas as pl
from jax.experimental.pallas import tpu as pltpu

D = 1024
CHUNK = 128
N_GROUPS = 4
GROUP = D // N_GROUPS
CONV_W = 31
HALO = CONV_W - 1
HALO_PAD = 32
LANES = 128
LANE_TILES = D // LANES
D_FF = 2816
EPS = 1e-6
TM = 256
SUB = CHUNK
N_SUB = TM // SUB
CHAIN_LAG = 1
SUBLANES = 8
FFN_CHUNKS = ((0, 1536), (1536, 1280))
HALO_TILES = HALO_PAD // SUBLANES
VMEM_LIMIT_BYTES = 62 * 1024 * 1024

F32 = jnp.float32
BF16 = jnp.bfloat16


def _rms(x, g):
    ms = jnp.mean(x * x, axis=-1, keepdims=True)
    return x * lax.rsqrt(ms + EPS) * g


def _ln(x, g, b):
    mu = jnp.mean(x, axis=-1, keepdims=True)
    xc = x - mu
    var = jnp.mean(xc * xc, axis=-1, keepdims=True)
    return xc * lax.rsqrt(var + EPS) * g + b


def _dot(a, b):
    return jnp.dot(a, b, preferred_element_type=F32)


def _pack_rows(w):
    return w.astype(BF16)


def _unpacked(w_ref, rows, cols):
    return w_ref[rows, cols]


def _mod_kernel(c_ref, w_ref, b_ref, o_ref):
    c = c_ref[...]
    s = (c * jax.nn.sigmoid(c)).astype(BF16)
    o_ref[...] = _dot(s, w_ref[...].astype(BF16)) + b_ref[...]


def _drive(chains, lag):
    live = list(chains)
    step = 0
    while live:
        for c in chains[:step // lag + 1]:
            if c in live:
                try:
                    next(c)
                except StopIteration:
                    live.remove(c)
        step += 1


def _layer_kernel(sample, *refs):
    if sample:
        (x_ref, mod_ref, cache_ref, w1_ref, w2_ref, *rest) = refs
    else:
        (x_ref, mod_ref, cwt_ref, cbt_ref, *rest) = refs
    (g1_ref, w_in_ref, b_in_ref, lnv_g_ref, lnv_b_ref, wmix_ref, bmix_ref,
     w_pa_ref, b_pa_ref, conv_b_ref, lnc_g_ref, lnc_b_ref, w_pb_ref, b_pb_ref,
     w_o_ref, g2_ref, w_fi_ref, w_fo_ref, gf_ref, *rest) = rest
    if sample:
        y_ref, conv_out_ref, cv_out_ref = rest
    else:
        y_ref, conv_out_ref, cv_out_ref, *scratch = rest
        ext_refs, cvz_refs = scratch[:N_SUB], scratch[N_SUB:]

        @pl.when(pl.program_id(1) == 0)
        def _():
            ext_refs[0][0:HALO_TILES] = jnp.zeros((HALO_TILES, SUBLANES * LANE_TILES, LANES), F32)

    nt = SUB // SUBLANES
    all_rows = slice(0, D)

    def row_idx(tile, j):
        return (tile, pl.ds(j, LANE_TILES, stride=SUBLANES), slice(None))

    def lane_tiles_to_rows(ref, lo, n):
        return jnp.concatenate(
            [ref[lo:lo + n, SUBLANES * s:SUBLANES * (s + 1), :].reshape(n * SUBLANES, LANES)
             for s in range(LANE_TILES)], axis=1)

    def rows_to_lane_tiles(ref, lo, val):
        n = val.shape[0] // SUBLANES
        for s in range(LANE_TILES):
            ref[lo:lo + n, SUBLANES * s:SUBLANES * (s + 1), :] = (
                val[:, LANES * s:LANES * (s + 1)].reshape(n, SUBLANES, LANES))

    def chain(h):
        r0 = h * SUB
        x = x_ref[r0:r0 + SUB, :]
        if sample:
            def mod_row(i):
                r = mod_ref[h * nt:(h + 1) * nt, i:i + 1, :]
                return jnp.broadcast_to(r, (nt, SUBLANES, D)).reshape(SUB, D)
        else:
            def mod_row(i):
                return mod_ref[i:i + 1, :]
        sh1, sc1, gt1, sh2, sc2, gt2 = [mod_row(i) for i in range(6)]
        hb = (_rms(x, g1_ref[...]) * (1.0 + sc1) + sh1).astype(BF16)
        yield

        p = [_dot(hb, _unpacked(w_in_ref, all_rows, slice(k * D, (k + 1) * D)))
             + b_in_ref[:, k * D:(k + 1) * D] for k in range(6)]
        yield

        u = jax.nn.gelu(p[0])
        v = _ln(jax.nn.gelu(p[1]), lnv_g_ref[...], lnv_b_ref[...])
        if sample:
            cv_out_ref[r0:r0 + SUB, :] = v
        elif h == N_SUB - 1:
            cv_out_ref[...] = v
        vb = v.astype(BF16)
        yield

        mixed = jnp.concatenate(
            [_dot(wmix_ref[g], vb[:, g * GROUP:(g + 1) * GROUP]) for g in range(N_GROUPS)],
            axis=1) + bmix_ref[...]
        yield

        ua = (u * mixed).astype(BF16)
        glu = p[2] * jax.nn.sigmoid(p[3])
        if sample:
            seqs = slice(h * nt, (h + 1) * nt)
            glu3 = glu.reshape(nt, SUBLANES, D)
            conv_out_ref[seqs, 0:HALO - SUBLANES, :] = cache_ref[seqs, SUBLANES:HALO, :]
            conv_out_ref[seqs, HALO - SUBLANES:HALO, :] = glu3
            acc = jnp.broadcast_to(conv_b_ref[...], (SUBLANES, D))[None]
            for j in range(HALO):
                acc = acc + cache_ref[seqs, j:j + 1, :] * w1_ref[j][None]
            for s in range(SUBLANES):
                acc = acc + glu3[:, s:s + 1, :] * w2_ref[s][None]
            cv = acc.reshape(SUB, D)
        else:
            ext, cvz = ext_refs[h], cvz_refs[h]
            rows_to_lane_tiles(ext, HALO_TILES, glu)
            rows_to_lane_tiles(ext_refs[(h + 1) % N_SUB], 0, glu[SUB - HALO_PAD:, :])
            if h == N_SUB - 1:
                conv_out_ref[...] = glu[SUB - HALO:, :]
            for i in range(nt):
                accs = [cbt_ref[...]] * SUBLANES
                for m in range(HALO_PAD - HALO, HALO_PAD + SUBLANES):
                    a_, j = divmod(m, SUBLANES)
                    z = ext[row_idx(i + a_, j)]
                    for r in range(SUBLANES):
                        k = m - r - (HALO_PAD - HALO)
                        if 0 <= k < CONV_W:
                            accs[r] = accs[r] + z * cwt_ref[SUBLANES * k:SUBLANES * (k + 1), :]
                for r in range(SUBLANES):
                    cvz[row_idx(i, r)] = accs[r]
            cv = lane_tiles_to_rows(cvz, 0, nt)
        cn = _ln(cv, lnc_g_ref[...], lnc_b_ref[...])
        cnb = (cn * jax.nn.sigmoid(cn)).astype(BF16)
        yield

        sq = slice(0, D)
        a = _dot(ua, _unpacked(w_pa_ref, all_rows, sq)) + b_pa_ref[...]
        bb = _dot(cnb, _unpacked(w_pb_ref, all_rows, sq)) + b_pb_ref[...]
        yield

        merged = (jax.nn.sigmoid(p[4]) * a + jax.nn.sigmoid(p[5]) * bb).astype(BF16)
        yield

        x1 = x + gt1 * _dot(merged, _unpacked(w_o_ref, all_rows, sq))
        yield

        h2 = (_rms(x1, g2_ref[...]) * (1.0 + sc2) + sh2).astype(BF16)
        yield

        ffn = None
        for lo, w in FFN_CHUNKS:
            gate = _dot(h2, _unpacked(w_fi_ref, all_rows, slice(lo, lo + w)))
            up = _dot(h2, _unpacked(w_fi_ref, all_rows, slice(D_FF + lo, D_FF + lo + w)))
            yield
            act = (gate * jax.nn.sigmoid(gate) * up).astype(BF16)
            yield
            part = _dot(act, _unpacked(w_fo_ref, slice(lo, lo + w), sq))
            ffn = part if ffn is None else ffn + part
        yield

        x2 = x1 + gt2 * ffn
        y_ref[r0:r0 + SUB, :] = _rms(x2, gf_ref[...])

    _drive([chain(h) for h in range(N_SUB)], CHAIN_LAG)


def _resident(shape):
    nd = len(shape)
    return pl.BlockSpec(shape, lambda *_: (0,) * nd, pipeline_mode=pl.Buffered(1))


def _modulation(c_all, w_ada, b_ada):
    n = c_all.shape[0]
    nb = 4
    bw = 6 * D // nb
    return pl.pallas_call(
        _mod_kernel,
        grid=(nb,),
        in_specs=[
            pl.BlockSpec((n, D), lambda i: (0, 0)),
            pl.BlockSpec((D, bw), lambda i: (0, i)),
            pl.BlockSpec((1, bw), lambda i: (0, i)),
        ],
        out_specs=pl.BlockSpec((n, bw), lambda i: (0, i)),
        out_shape=jax.ShapeDtypeStruct((n, 6 * D), F32),
        name="adaln_mod",
    )(c_all, w_ada, b_ada)


def kernel(x_prompt, x_sample, c_prompt, c_sample, cache_conv, w_ada, b_ada, g_norm1,
           w_in, b_in, ln_v_g, ln_v_b, w_s, b_s, w_pa, b_pa, conv_w, conv_b,
           ln_c_g, ln_c_b, w_pb, b_pb, w_o, g_norm2, w_ffn_in, w_ffn_out, g_final):
    nb_p, seq, _ = x_prompt.shape
    nb_s, dec_seq, _ = x_sample.shape
    assert dec_seq == SUBLANES and seq % TM == 0 and (nb_s * dec_seq) % TM == 0
    assert w_ada.shape[0] == 1

    mod = _modulation(jnp.concatenate([c_prompt, c_sample], axis=0), w_ada[0],
                      b_ada[0][None]).reshape(nb_p + nb_s, 6, D)
    mod_p, mod_s = mod[:nb_p], mod[nb_p:]

    row = lambda a: a[0][None].astype(F32)
    tril = jnp.tril(jnp.ones((CHUNK, CHUNK), bool))
    wmix_p = jnp.where(tril[None], w_s[0], 0.0).astype(BF16)
    bmix_p = jnp.repeat(b_s[0].T, GROUP, axis=1)
    reps = CHUNK // dec_seq
    tril_s = jnp.tril(jnp.ones((dec_seq, dec_seq), bool))
    w_small = jnp.where(tril_s[None], w_s[0][:, :dec_seq, :dec_seq], 0.0)
    wmix_s = jax.vmap(lambda m: jnp.kron(jnp.eye(reps, dtype=F32), m))(w_small).astype(BF16)
    bmix_s = jnp.tile(jnp.repeat(b_s[0][:, :dec_seq].T, GROUP, axis=1), (reps, 1))
    cw = conv_w[0]
    cwt = cw.reshape(CONV_W * LANE_TILES, LANES)
    cbt = conv_b[0].reshape(LANE_TILES, LANES)
    w1 = jnp.stack([jnp.pad(cw[:HALO - r], ((r, 0), (0, 0))) for r in range(dec_seq)], axis=1)
    w2 = jnp.stack([jnp.pad(cw[HALO - r:], ((0, dec_seq - 1 - r), (0, 0))) for r in range(dec_seq)],
                   axis=1)

    shared = [
        row(g_norm1), _pack_rows(w_in[0]), row(b_in), row(ln_v_g), row(ln_v_b),
        None, None, _pack_rows(w_pa[0]), row(b_pa), row(conv_b), row(ln_c_g),
        row(ln_c_b), _pack_rows(w_pb[0]), row(b_pb), _pack_rows(w_o[0]),
        row(g_norm2), _pack_rows(w_ffn_in[0]), _pack_rows(w_ffn_out[0]),
        g_final[None].astype(F32),
    ]

    def weights(wmix, bmix):
        return shared[:5] + [wmix, bmix] + shared[7:]

    params = pltpu.CompilerParams(
        dimension_semantics=("arbitrary", "arbitrary"),
        vmem_limit_bytes=VMEM_LIMIT_BYTES)

    wp = weights(wmix_p, bmix_p)
    lane_tile_shape = (SUBLANES * LANE_TILES, LANES)
    y_p, conv_p, cv_p = pl.pallas_call(
        functools.partial(_layer_kernel, False),
        grid=(nb_p, seq // TM),
        in_specs=[
            pl.BlockSpec((None, TM, D), lambda b, j: (b, j, 0)),
            pl.BlockSpec((None, 6, D), lambda b, j: (b, 0, 0)),
            _resident(cwt.shape), _resident(cbt.shape),
        ] + [_resident(w.shape) for w in wp],
        out_specs=[
            pl.BlockSpec((None, TM, D), lambda b, j: (b, j, 0)),
            pl.BlockSpec((None, HALO, D), lambda b, j: (b, 0, 0)),
            pl.BlockSpec((None, CHUNK, D), lambda b, j: (b, 0, 0)),
        ],
        out_shape=[
            jax.ShapeDtypeStruct((nb_p, seq, D), F32),
            jax.ShapeDtypeStruct((nb_p, HALO, D), F32),
            jax.ShapeDtypeStruct((nb_p, CHUNK, D), F32),
        ],
        scratch_shapes=(
            [pltpu.VMEM((HALO_TILES + SUB // SUBLANES,) + lane_tile_shape, F32)] * N_SUB
            + [pltpu.VMEM((SUB // SUBLANES,) + lane_tile_shape, F32)] * N_SUB),
        compiler_params=params,
        name="layer_prompt",
    )(x_prompt, mod_p, cwt, cbt, *wp)

    ws = weights(wmix_s, bmix_s)
    rows_s = nb_s * dec_seq
    ns = TM // dec_seq
    once = pl.Buffered(1)
    y_s, conv_s, cv_s = pl.pallas_call(
        functools.partial(_layer_kernel, True),
        grid=(1, rows_s // TM),
        in_specs=[
            pl.BlockSpec((TM, D), lambda b, j: (j, 0)),
            pl.BlockSpec((ns, 6, D), lambda b, j: (j, 0, 0)),
            pl.BlockSpec((ns, HALO, D), lambda b, j: (j, 0, 0), pipeline_mode=once),
            _resident(w1.shape), _resident(w2.shape),
        ] + [_resident(w.shape) for w in ws],
        out_specs=[
            pl.BlockSpec((TM, D), lambda b, j: (j, 0)),
            pl.BlockSpec((ns, HALO, D), lambda b, j: (j, 0, 0), pipeline_mode=once),
            pl.BlockSpec((TM, D), lambda b, j: (j, 0)),
        ],
        out_shape=[
            jax.ShapeDtypeStruct((rows_s, D), F32),
            jax.ShapeDtypeStruct((nb_s, HALO, D), F32),
            jax.ShapeDtypeStruct((rows_s, D), F32),
        ],
        compiler_params=params,
        name="layer_sample",
    )(x_sample.reshape(rows_s, D), mod_s, cache_conv[0], w1, w2, *ws)

    return (y_p, y_s.reshape(nb_s, dec_seq, D), conv_p[None], conv_s[None],
            cv_p[None], cv_s.reshape(1, nb_s, dec_seq, D))
```

```python
import functools

import jax
import jax.numpy as jnp
from jax import lax
from jax.experimental import pallas as pl
from jax.experimental.pallas import tpu as pltpu

D = 1024
CHUNK = 128
N_GROUPS = 4
GROUP = D // N_GROUPS
CONV_W = 31
HALO = CONV_W - 1
HALO_PAD = 32
LANES = 128
LANE_TILES = D // LANES
D_FF = 2816
EPS = 1e-6
TM = 256
SUB = CHUNK
N_SUB = TM // SUB
CHAIN_LAG = 1
SUBLANES = 8
FFN_CHUNKS = ((0, 1024), (1024, 1024), (2048, 768))
HALO_TILES = HALO_PAD // SUBLANES
VMEM_LIMIT_BYTES = 62 * 1024 * 1024

F32 = jnp.float32
BF16 = jnp.bfloat16


def _rms(x, g):
    ms = jnp.mean(x * x, axis=-1, keepdims=True)
    return x * lax.rsqrt(ms + EPS) * g


def _ln(x, g, b):
    mu = jnp.mean(x, axis=-1, keepdims=True)
    xc = x - mu
    var = jnp.mean(xc * xc, axis=-1, keepdims=True)
    return xc * lax.rsqrt(var + EPS) * g + b


def _dot(a, b):
    return jnp.dot(a, b, preferred_element_type=F32)


def _pack_rows(w):
    return w.astype(BF16)


def _unpacked(w_ref, rows, cols):
    return w_ref[rows, cols]


def _mod_kernel(c_ref, w_ref, b_ref, o_ref):
    c = c_ref[...]
    s = (c * jax.nn.sigmoid(c)).astype(BF16)
    o_ref[...] = _dot(s, w_ref[...].astype(BF16)) + b_ref[...]


def _drive(chains, lag):
    live = list(chains)
    step = 0
    while live:
        for c in chains[:step // lag + 1]:
            if c in live:
                try:
                    next(c)
                except StopIteration:
                    live.remove(c)
        step += 1


def _layer_kernel(sample, *refs):
    if sample:
        (x_ref, mod_ref, cache_ref, w1_ref, w2_ref, *rest) = refs
    else:
        (x_ref, mod_ref, cwt_ref, cbt_ref, *rest) = refs
    (g1_ref, w_in_ref, b_in_ref, lnv_g_ref, lnv_b_ref, wmix_ref, bmix_ref,
     w_pa_ref, b_pa_ref, conv_b_ref, lnc_g_ref, lnc_b_ref, w_pb_ref, b_pb_ref,
     w_o_ref, g2_ref, w_fi_ref, w_fo_ref, gf_ref, *rest) = rest
    if sample:
        y_ref, conv_out_ref, cv_out_ref = rest
    else:
        y_ref, conv_out_ref, cv_out_ref, *scratch = rest
        ext_refs, cvz_refs = scratch[:N_SUB], scratch[N_SUB:]

        @pl.when(pl.program_id(1) == 0)
        def _():
            ext_refs[0][0:HALO_TILES] = jnp.zeros((HALO_TILES, SUBLANES * LANE_TILES, LANES), F32)

    nt = SUB // SUBLANES
    all_rows = slice(0, D)

    def row_idx(tile, j):
        return (tile, pl.ds(j, LANE_TILES, stride=SUBLANES), slice(None))

    def lane_tiles_to_rows(ref, lo, n):
        return jnp.concatenate(
            [ref[lo:lo + n, SUBLANES * s:SUBLANES * (s + 1), :].reshape(n * SUBLANES, LANES)
             for s in range(LANE_TILES)], axis=1)

    def rows_to_lane_tiles(ref, lo, val):
        n = val.shape[0] // SUBLANES
        for s in range(LANE_TILES):
            ref[lo:lo + n, SUBLANES * s:SUBLANES * (s + 1), :] = (
                val[:, LANES * s:LANES * (s + 1)].reshape(n, SUBLANES, LANES))

    def chain(h):
        r0 = h * SUB
        x = x_ref[r0:r0 + SUB, :]
        if sample:
            def mod_row(i):
                r = mod_ref[h * nt:(h + 1) * nt, i:i + 1, :]
                return jnp.broadcast_to(r, (nt, SUBLANES, D)).reshape(SUB, D)
        else:
            def mod_row(i):
                return mod_ref[i:i + 1, :]
        sh1, sc1, gt1, sh2, sc2, gt2 = [mod_row(i) for i in range(6)]
        hb = (_rms(x, g1_ref[...]) * (1.0 + sc1) + sh1).astype(BF16)
        yield

        p = [_dot(hb, _unpacked(w_in_ref, all_rows, slice(k * D, (k + 1) * D)))
             + b_in_ref[:, k * D:(k + 1) * D] for k in range(6)]
        yield

        u = jax.nn.gelu(p[0])
        v = _ln(jax.nn.gelu(p[1]), lnv_g_ref[...], lnv_b_ref[...])
        if sample:
            cv_out_ref[r0:r0 + SUB, :] = v
        elif h == N_SUB - 1:
            cv_out_ref[...] = v
        vb = v.astype(BF16)
        yield

        mixed = jnp.concatenate(
            [_dot(wmix_ref[g], vb[:, g * GROUP:(g + 1) * GROUP]) for g in range(N_GROUPS)],
            axis=1) + bmix_ref[...]
        yield

        ua = (u * mixed).astype(BF16)
        glu = p[2] * jax.nn.sigmoid(p[3])
        if sample:
            seqs = slice(h * nt, (h + 1) * nt)
            glu3 = glu.reshape(nt, SUBLANES, D)
            conv_out_ref[seqs, 0:HALO - SUBLANES, :] = cache_ref[seqs, SUBLANES:HALO, :]
            conv_out_ref[seqs, HALO - SUBLANES:HALO, :] = glu3
            acc = jnp.broadcast_to(conv_b_ref[...], (SUBLANES, D))[None]
            for j in range(HALO):
                acc = acc + cache_ref[seqs, j:j + 1, :] * w1_ref[j][None]
            for s in range(SUBLANES):
                acc = acc + glu3[:, s:s + 1, :] * w2_ref[s][None]
            cv = acc.reshape(SUB, D)
        else:
            ext, cvz = ext_refs[h], cvz_refs[h]
            rows_to_lane_tiles(ext, HALO_TILES, glu)
            rows_to_lane_tiles(ext_refs[(h + 1) % N_SUB], 0, glu[SUB - HALO_PAD:, :])
            if h == N_SUB - 1:
                conv_out_ref[...] = glu[SUB - HALO:, :]
            for i in range(nt):
                accs = [cbt_ref[...]] * SUBLANES
                for m in range(HALO_PAD - HALO, HALO_PAD + SUBLANES):
                    a_, j = divmod(m, SUBLANES)
                    z = ext[row_idx(i + a_, j)]
                    for r in range(SUBLANES):
                        k = m - r - (HALO_PAD - HALO)
                        if 0 <= k < CONV_W:
                            accs[r] = accs[r] + z * cwt_ref[SUBLANES * k:SUBLANES * (k + 1), :]
                for r in range(SUBLANES):
                    cvz[row_idx(i, r)] = accs[r]
            cv = lane_tiles_to_rows(cvz, 0, nt)
        cn = _ln(cv, lnc_g_ref[...], lnc_b_ref[...])
        cnb = (cn * jax.nn.sigmoid(cn)).astype(BF16)
        yield

        sq = slice(0, D)
        a = _dot(ua, _unpacked(w_pa_ref, all_rows, sq)) + b_pa_ref[...]
        bb = _dot(cnb, _unpacked(w_pb_ref, all_rows, sq)) + b_pb_ref[...]
        yield

        merged = (jax.nn.sigmoid(p[4]) * a + jax.nn.sigmoid(p[5]) * bb).astype(BF16)
        yield

        x1 = x + gt1 * _dot(merged, _unpacked(w_o_ref, all_rows, sq))
        yield

        h2 = (_rms(x1, g2_ref[...]) * (1.0 + sc2) + sh2).astype(BF16)
        yield

        ffn = None
        for lo, w in FFN_CHUNKS:
            gate = _dot(h2, _unpacked(w_fi_ref, all_rows, slice(lo, lo + w)))
            up = _dot(h2, _unpacked(w_fi_ref, all_rows, slice(D_FF + lo, D_FF + lo + w)))
            yield
            act = (gate * jax.nn.sigmoid(gate) * up).astype(BF16)
            yield
            part = _dot(act, _unpacked(w_fo_ref, slice(lo, lo + w), sq))
            ffn = part if ffn is None else ffn + part
        yield

        x2 = x1 + gt2 * ffn
        y_ref[r0:r0 + SUB, :] = _rms(x2, gf_ref[...])

    _drive([chain(h) for h in range(N_SUB)], CHAIN_LAG)


def _resident(shape):
    nd = len(shape)
    return pl.BlockSpec(shape, lambda *_: (0,) * nd, pipeline_mode=pl.Buffered(1))


def _modulation(c_all, w_ada, b_ada):
    n = c_all.shape[0]
    nb = 4
    bw = 6 * D // nb
    return pl.pallas_call(
        _mod_kernel,
        grid=(nb,),
        in_specs=[
            pl.BlockSpec((n, D), lambda i: (0, 0)),
            pl.BlockSpec((D, bw), lambda i: (0, i)),
            pl.BlockSpec((1, bw), lambda i: (0, i)),
        ],
        out_specs=pl.BlockSpec((n, bw), lambda i: (0, i)),
        out_shape=jax.ShapeDtypeStruct((n, 6 * D), F32),
        name="adaln_mod",
    )(c_all, w_ada, b_ada)


def kernel(x_prompt, x_sample, c_prompt, c_sample, cache_conv, w_ada, b_ada, g_norm1,
           w_in, b_in, ln_v_g, ln_v_b, w_s, b_s, w_pa, b_pa, conv_w, conv_b,
           ln_c_g, ln_c_b, w_pb, b_pb, w_o, g_norm2, w_ffn_in, w_ffn_out, g_final):
    nb_p, seq, _ = x_prompt.shape
    nb_s, dec_seq, _ = x_sample.shape
    assert dec_seq == SUBLANES and seq % TM == 0 and (nb_s * dec_seq) % TM == 0
    assert w_ada.shape[0] == 1

    mod = _modulation(jnp.concatenate([c_prompt, c_sample], axis=0), w_ada[0],
                      b_ada[0][None]).reshape(nb_p + nb_s, 6, D)
    mod_p, mod_s = mod[:nb_p], mod[nb_p:]

    row = lambda a: a[0][None].astype(F32)
    tril = jnp.tril(jnp.ones((CHUNK, CHUNK), bool))
    wmix_p = jnp.where(tril[None], w_s[0], 0.0).astype(BF16)
    bmix_p = jnp.repeat(b_s[0].T, GROUP, axis=1)
    reps = CHUNK // dec_seq
    tril_s = jnp.tril(jnp.ones((dec_seq, dec_seq), bool))
    w_small = jnp.where(tril_s[None], w_s[0][:, :dec_seq, :dec_seq], 0.0)
    wmix_s = jax.vmap(lambda m: jnp.kron(jnp.eye(reps, dtype=F32), m))(w_small).astype(BF16)
    bmix_s = jnp.tile(jnp.repeat(b_s[0][:, :dec_seq].T, GROUP, axis=1), (reps, 1))
    cw = conv_w[0]
    cwt = cw.reshape(CONV_W * LANE_TILES, LANES)
    cbt = conv_b[0].reshape(LANE_TILES, LANES)
    w1 = jnp.stack([jnp.pad(cw[:HALO - r], ((r, 0), (0, 0))) for r in range(dec_seq)], axis=1)
    w2 = jnp.stack([jnp.pad(cw[HALO - r:], ((0, dec_seq - 1 - r), (0, 0))) for r in range(dec_seq)],
                   axis=1)

    shared = [
        row(g_norm1), _pack_rows(w_in[0]), row(b_in), row(ln_v_g), row(ln_v_b),
        None, None, _pack_rows(w_pa[0]), row(b_pa), row(conv_b), row(ln_c_g),
        row(ln_c_b), _pack_rows(w_pb[0]), row(b_pb), _pack_rows(w_o[0]),
        row(g_norm2), _pack_rows(w_ffn_in[0]), _pack_rows(w_ffn_out[0]),
        g_final[None].astype(F32),
    ]

    def weights(wmix, bmix):
        return shared[:5] + [wmix, bmix] + shared[7:]

    params = pltpu.CompilerParams(
        dimension_semantics=("arbitrary", "arbitrary"),
        vmem_limit_bytes=VMEM_LIMIT_BYTES)

    wp = weights(wmix_p, bmix_p)
    lane_tile_shape = (SUBLANES * LANE_TILES, LANES)
    y_p, conv_p, cv_p = pl.pallas_call(
        functools.partial(_layer_kernel, False),
        grid=(nb_p, seq // TM),
        in_specs=[
            pl.BlockSpec((None, TM, D), lambda b, j: (b, j, 0)),
            pl.BlockSpec((None, 6, D), lambda b, j: (b, 0, 0)),
            _resident(cwt.shape), _resident(cbt.shape),
        ] + [_resident(w.shape) for w in wp],
        out_specs=[
            pl.BlockSpec((None, TM, D), lambda b, j: (b, j, 0)),
            pl.BlockSpec((None, HALO, D), lambda b, j: (b, 0, 0)),
            pl.BlockSpec((None, CHUNK, D), lambda b, j: (b, 0, 0)),
        ],
        out_shape=[
            jax.ShapeDtypeStruct((nb_p, seq, D), F32),
            jax.ShapeDtypeStruct((nb_p, HALO, D), F32),
            jax.ShapeDtypeStruct((nb_p, CHUNK, D), F32),
        ],
        scratch_shapes=(
            [pltpu.VMEM((HALO_TILES + SUB // SUBLANES,) + lane_tile_shape, F32)] * N_SUB
            + [pltpu.VMEM((SUB // SUBLANES,) + lane_tile_shape, F32)] * N_SUB),
        compiler_params=params,
        name="layer_prompt",
    )(x_prompt, mod_p, cwt, cbt, *wp)

    ws = weights(wmix_s, bmix_s)
    rows_s = nb_s * dec_seq
    ns = TM // dec_seq
    once = pl.Buffered(1)
    y_s, conv_s, cv_s = pl.pallas_call(
        functools.partial(_layer_kernel, True),
        grid=(1, rows_s // TM),
        in_specs=[
            pl.BlockSpec((TM, D), lambda b, j: (j, 0)),
            pl.BlockSpec((ns, 6, D), lambda b, j: (j, 0, 0)),
            pl.BlockSpec((None, ns, HALO, D), lambda b, j: (0, j, 0, 0), pipeline_mode=once),
            _resident(w1.shape), _resident(w2.shape),
        ] + [_resident(w.shape) for w in ws],
        out_specs=[
            pl.BlockSpec((TM, D), lambda b, j: (j, 0)),
            pl.BlockSpec((None, ns, HALO, D), lambda b, j: (0, j, 0, 0), pipeline_mode=once),
            pl.BlockSpec((TM, D), lambda b, j: (j, 0)),
        ],
        out_shape=[
            jax.ShapeDtypeStruct((rows_s, D), F32),
            jax.ShapeDtypeStruct((1, nb_s, HALO, D), F32),
            jax.ShapeDtypeStruct((rows_s, D), F32),
        ],
        compiler_params=params,
        name="layer_sample",
    )(x_sample.reshape(rows_s, D), mod_s, cache_conv, w1, w2, *ws)

    return (y_p, y_s.reshape(nb_s, dec_seq, D), conv_p[None], conv_s,
            cv_p[None], cv_s.reshape(1, nb_s, dec_seq, D))
```
